```python
import math, functools
import jax, jax.numpy as jnp
from jax import lax
import numpy as np

D_MODEL = 1024
BATCH = 32
SEQ = 2048
DEPTH = 2
DEC_BATCH = 128
DEC_SEQ = 8
PAST_LEN = 16384
PAGE_SIZE = 128

N_MIXERS = 2
CONV_WIDTH = 3
D_CONV = D_MODEL
N_HEADS = 8
QK_NOPE = 128
QK_ROPE = 64
QK_HEAD = QK_NOPE + QK_ROPE
V_HEAD = 128
KV_LORA = D_MODEL // 4
Q_LORA = 3 * D_MODEL // 8
D_MLA = N_HEADS * V_HEAD
ROPE_THETA = 10000.0
N_MEM = 256
MEM_HEADS = 4
MEM_HEAD_DIM = 128
D_MEM = MEM_HEADS * MEM_HEAD_DIM
D_BRANCH_CONV = D_CONV + D_MEM
D_BRANCH_MLA = D_MLA + D_MEM
CONV_SPLITS = [D_CONV, 2 * D_CONV, 3 * D_CONV, 3 * D_CONV + D_MEM]
MLA_SPLITS = [Q_LORA, Q_LORA + KV_LORA, Q_LORA + KV_LORA + QK_ROPE, Q_LORA + KV_LORA + QK_ROPE + D_MEM]
EPS = 1e-6
SM_SCALE = QK_HEAD ** -0.5
MEM_SCALE = MEM_HEAD_DIM ** -0.5
Q_BLOCK = 128
KV_BLOCK_PAGES_MAX = 16

kernel_name = 'hybrid_shortconv_mla_memory_step'


def rms_norm(x, g):
    xf = x.astype(jnp.float32)
    y = xf * lax.rsqrt(jnp.mean(xf * xf, axis=-1, keepdims=True) + EPS)
    return (y * g.astype(jnp.float32)).astype(x.dtype)


def rope_tables(pos):
    inv = 1.0 / (ROPE_THETA ** (jnp.arange(0, QK_ROPE, 2, dtype=jnp.float32) / QK_ROPE))
    ang = pos.astype(jnp.float32)[:, None] * inv[None, :]
    return jnp.cos(ang), jnp.sin(ang)


def apply_rope(x, cos, sin):
    half = x.shape[-1] // 2
    xf = x.astype(jnp.float32)
    x1, x2 = xf[..., :half], xf[..., half:]
    return jnp.concatenate([x1 * cos - x2 * sin, x1 * sin + x2 * cos], axis=-1).astype(x.dtype)


def mem_kv(mem, norm_g, w_kv, k_g):
    b, m, _ = mem.shape
    kv = rms_norm(mem, norm_g) @ w_kv
    k = kv[..., :D_MEM].reshape(b, m, MEM_HEADS, MEM_HEAD_DIM)
    v = kv[..., D_MEM:].reshape(b, m, MEM_HEADS, MEM_HEAD_DIM)
    return rms_norm(k, k_g), v


def mem_attend(q, k, v):
    s = jnp.einsum('bthd,bmhd->bhtm', q, k, preferred_element_type=jnp.float32) * MEM_SCALE
    p = jax.nn.softmax(s, axis=-1)
    o = jnp.einsum('bhtm,bmhd->bthd', p.astype(v.dtype), v)
    return o.reshape(q.shape[0], q.shape[1], D_MEM)


def conv_layer(x, hist, mem_k, mem_v, mem_q_g, norm_g, w_in, conv_w, w_out):
    b, t, _ = x.shape
    proj = rms_norm(x, norm_g) @ w_in
    u, gb, gc, q_mem, z = jnp.split(proj, CONV_SPLITS, axis=-1)
    full = jnp.concatenate([hist.astype(x.dtype), gc * u], axis=1)
    conv = sum(conv_w[j] * full[:, j:j + t] for j in range(CONV_WIDTH))
    y_conv = gb * conv
    q = rms_norm(q_mem.reshape(b, t, MEM_HEADS, MEM_HEAD_DIM), mem_q_g)
    y_mem = mem_attend(q, mem_k, mem_v)
    out = jnp.concatenate([y_conv, y_mem], axis=-1) * jax.nn.silu(z)
    return x + out @ w_out, full[:, t:]


def mla_scores(q_abs, q_pe, ckv, kpe, ks):
    s = (jnp.einsum('bqhc,bkc->bhqk', q_abs, ckv, preferred_element_type=jnp.float32)
         + jnp.einsum('bqhr,bkr->bhqk', q_pe, kpe, preferred_element_type=jnp.float32))
    return s * (jnp.transpose(ks.astype(jnp.float32), (0, 2, 1))[:, :, None, :] * SM_SCALE)


def mla_attend_prompt(q_abs, q_pe, ckv, kpe, ks):
    t = q_abs.shape[1]
    outs = []
    for qb in range(t // Q_BLOCK):
        lo, hi = qb * Q_BLOCK, (qb + 1) * Q_BLOCK
        s = mla_scores(q_abs[:, lo:hi], q_pe[:, lo:hi], ckv[:, :hi], kpe[:, :hi], ks[:, :hi])
        mask = (lo + jnp.arange(Q_BLOCK))[:, None] >= jnp.arange(hi)[None, :]
        p = jax.nn.softmax(jnp.where(mask[None, None], s, -jnp.inf), axis=-1)
        outs.append(jnp.einsum('bhqk,bkc->bqhc', p.astype(ckv.dtype), ckv[:, :hi]))
    return jnp.concatenate(outs, axis=1)


def mla_attend_sample(q_abs, q_pe, ckv, kpe, ks, cache_ckv, cache_kpe, cache_ks, page_table):
    b, t = q_abs.shape[0], q_abs.shape[1]
    s = mla_scores(q_abs, q_pe, ckv, kpe, ks)
    s = jnp.where(jnp.tril(jnp.ones((t, t), dtype=bool))[None, None], s, -jnp.inf)
    m = jnp.max(s, axis=-1)
    p = jnp.exp(s - m[..., None])
    l = jnp.sum(p, axis=-1)
    acc = jnp.einsum('bhqk,bkc->bhqc', p, ckv, preferred_element_type=jnp.float32)
    n_pages = page_table.shape[1]
    ppb = math.gcd(n_pages, KV_BLOCK_PAGES_MAX)
    blocks = jnp.transpose(page_table.reshape(b, n_pages // ppb, ppb), (1, 0, 2))

    def step(carry, pages):
        m, l, acc = carry
        ck = cache_ckv[pages].reshape(b, ppb * PAGE_SIZE, KV_LORA)
        kp = cache_kpe[pages].reshape(b, ppb * PAGE_SIZE, QK_ROPE)
        kk = cache_ks[pages].reshape(b, ppb * PAGE_SIZE, N_HEADS)
        s = mla_scores(q_abs, q_pe, ck, kp, kk)
        m_new = jnp.maximum(m, jnp.max(s, axis=-1))
        alpha = jnp.exp(m - m_new)
        p = jnp.exp(s - m_new[..., None])
        l = l * alpha + jnp.sum(p, axis=-1)
        acc = acc * alpha[..., None] + jnp.einsum('bhqk,bkc->bhqc', p, ck, preferred_element_type=jnp.float32)
        return (m_new, l, acc), None

    (m, l, acc), _ = lax.scan(step, (m, l, acc), blocks)
    return jnp.transpose(acc / l[..., None], (0, 2, 1, 3)).astype(q_abs.dtype)


def mla_layer(x, pos, attend, mem_k, mem_v, mem_q_g, norm_g, w_in, q_lora_g, w_uq, ckv_g, w_uk, w_uv, q_g, k_g, w_out):
    b, t, _ = x.shape
    proj = rms_norm(x, norm_g) @ w_in
    cq, ckv_raw, kpe_raw, q_mem, z = jnp.split(proj, MLA_SPLITS, axis=-1)
    q = (rms_norm(cq, q_lora_g) @ w_uq).reshape(b, t, N_HEADS, QK_HEAD)
    qn = rms_norm(q, q_g)
    ckv = rms_norm(ckv_raw, ckv_g)
    k_nope = jnp.einsum('btc,chd->bthd', ckv, w_uk)
    ms = (jnp.sum(jnp.square(k_nope.astype(jnp.float32)), axis=-1)
          + jnp.sum(jnp.square(kpe_raw.astype(jnp.float32)), axis=-1)[..., None]) / QK_HEAD
    kscale = lax.rsqrt(ms + EPS).astype(x.dtype)
    cos, sin = rope_tables(pos)
    kpe = apply_rope(kpe_raw * k_g[QK_NOPE:], cos, sin)
    q_abs = jnp.einsum('bthd,chd->bthc', qn[..., :QK_NOPE] * k_g[:QK_NOPE], w_uk)
    q_pe = apply_rope(qn[..., QK_NOPE:], cos[:, None, :], sin[:, None, :])
    o_lat = attend(q_abs, q_pe, ckv, kpe, kscale)
    y_mla = jnp.einsum('bthc,chd->bthd', o_lat, w_uv).reshape(b, t, D_MLA)
    qm = rms_norm(q_mem.reshape(b, t, MEM_HEADS, MEM_HEAD_DIM), mem_q_g)
    y_mem = mem_attend(qm, mem_k, mem_v)
    out = jnp.concatenate([y_mla, y_mem], axis=-1) * jax.nn.silu(z)
    return x + out @ w_out, ckv, kpe, kscale


def setup_inputs(seed: int = 0) -> dict:
    key = jax.random.key(seed)
    ks = jax.random.split(key, 32)
    f32 = jnp.float32
    n_pages = PAST_LEN // PAGE_SIZE
    n_used = DEC_BATCH * n_pages
    n_pool = n_used + n_used // 4

    def nrm(k, shape, scale=1.0):
        return jax.random.normal(k, shape, f32) * scale

    def gain(k, shape):
        return 1.0 + 0.02 * jax.random.normal(k, shape, f32)

    page_table = jax.random.permutation(ks[0], n_pool)[:n_used].reshape(DEC_BATCH, n_pages).astype(jnp.int32)
    return {
        'x_prompt': nrm(ks[1], (BATCH, SEQ, D_MODEL)),
        'x_sample': nrm(ks[2], (DEC_BATCH, DEC_SEQ, D_MODEL)),
        'mem_prompt': nrm(ks[3], (BATCH, N_MEM, D_MODEL)),
        'state_conv': nrm(ks[4], (DEC_BATCH, CONV_WIDTH - 1, D_CONV)),
        'cache_ckv': nrm(ks[5], (n_pool, PAGE_SIZE, KV_LORA)),
        'cache_kpe': nrm(ks[6], (n_pool, PAGE_SIZE, QK_ROPE)),
        'cache_kscale': jax.random.uniform(ks[7], (n_pool, PAGE_SIZE, N_HEADS), f32, 0.8, 1.2),
        'cache_mem_k': nrm(ks[8], (DEPTH, DEC_BATCH, N_MEM, MEM_HEADS, MEM_HEAD_DIM)),
        'cache_mem_v': nrm(ks[9], (DEPTH, DEC_BATCH, N_MEM, MEM_HEADS, MEM_HEAD_DIM)),
        'page_table': page_table,
        'conv_norm_g': gain(ks[10], (D_MODEL,)),
        'conv_w_in': nrm(ks[11], (D_MODEL, 3 * D_CONV + D_MEM + D_BRANCH_CONV), D_MODEL ** -0.5),
        'conv_w': nrm(ks[12], (CONV_WIDTH, D_CONV), CONV_WIDTH ** -0.5),
        'conv_w_out': nrm(ks[13], (D_BRANCH_CONV, D_MODEL), D_BRANCH_CONV ** -0.5),
        'mla_norm_g': gain(ks[14], (D_MODEL,)),
        'mla_w_in': nrm(ks[15], (D_MODEL, Q_LORA + KV_LORA + QK_ROPE + D_MEM + D_BRANCH_MLA), D_MODEL ** -0.5),
        'mla_q_lora_g': gain(ks[16], (Q_LORA,)),
        'mla_w_uq': nrm(ks[17], (Q_LORA, N_HEADS * QK_HEAD), Q_LORA ** -0.5),
        'mla_ckv_g': gain(ks[18], (KV_LORA,)),
        'mla_w_uk': nrm(ks[19], (KV_LORA, N_HEADS, QK_NOPE), KV_LORA ** -0.5),
        'mla_w_uv': nrm(ks[20], (KV_LORA, N_HEADS, V_HEAD), KV_LORA ** -0.5),
        'mla_q_g': gain(ks[21], (QK_HEAD,)),
        'mla_k_g': gain(ks[22], (QK_HEAD,)),
        'mla_w_out': nrm(ks[23], (D_BRANCH_MLA, D_MODEL), D_BRANCH_MLA ** -0.5),
        'mem_norm_g': gain(ks[24], (DEPTH, D_MODEL)),
        'mem_w_kv': nrm(ks[25], (DEPTH, D_MODEL, 2 * D_MEM), D_MODEL ** -0.5),
        'mem_q_g': gain(ks[26], (DEPTH, MEM_HEAD_DIM)),
        'mem_k_g': gain(ks[27], (DEPTH, MEM_HEAD_DIM)),
    }


def reference(x_prompt, x_sample, mem_prompt, state_conv, cache_ckv, cache_kpe, cache_kscale, cache_mem_k,
              cache_mem_v, page_table, conv_norm_g, conv_w_in, conv_w, conv_w_out, mla_norm_g, mla_w_in,
              mla_q_lora_g, mla_w_uq, mla_ckv_g, mla_w_uk, mla_w_uv, mla_q_g, mla_k_g, mla_w_out,
              mem_norm_g, mem_w_kv, mem_q_g, mem_k_g):
    seq = x_prompt.shape[1]
    past = page_table.shape[1] * PAGE_SIZE
    pos_p = jnp.arange(seq, dtype=jnp.int32)
    pos_s = past + jnp.arange(x_sample.shape[1], dtype=jnp.int32)
    attend_s = functools.partial(mla_attend_sample, cache_ckv=cache_ckv, cache_kpe=cache_kpe,
                                 cache_ks=cache_kscale, page_table=page_table)
    y_p, y_s = x_prompt, x_sample
    mem_ks, mem_vs = [], []
    for i in range(DEPTH):
        mk_p, mv_p = mem_kv(mem_prompt, mem_norm_g[i], mem_w_kv[i], mem_k_g[i])
        mem_ks.append(mk_p)
        mem_vs.append(mv_p)
        mk_s, mv_s = cache_mem_k[i], cache_mem_v[i]
        if i % N_MIXERS == 0:
            hist_p = jnp.zeros((y_p.shape[0], CONV_WIDTH - 1, D_CONV), y_p.dtype)
            y_p, conv_p = conv_layer(y_p, hist_p, mk_p, mv_p, mem_q_g[i], conv_norm_g, conv_w_in, conv_w, conv_w_out)
            y_s, conv_s = conv_layer(y_s, state_conv, mk_s, mv_s, mem_q_g[i], conv_norm_g, conv_w_in, conv_w, conv_w_out)
        else:
            y_p, ckv_p, kpe_p, ksc_p = mla_layer(y_p, pos_p, mla_attend_prompt, mk_p, mv_p, mem_q_g[i],
                                                 mla_norm_g, mla_w_in, mla_q_lora_g, mla_w_uq, mla_ckv_g,
                                                 mla_w_uk, mla_w_uv, mla_q_g, mla_k_g, mla_w_out)
            y_s, ckv_s, kpe_s, ksc_s = mla_layer(y_s, pos_s, attend_s, mk_s, mv_s, mem_q_g[i],
                                                 mla_norm_g, mla_w_in, mla_q_lora_g, mla_w_uq, mla_ckv_g,
                                                 mla_w_uk, mla_w_uv, mla_q_g, mla_k_g, mla_w_out)
    new_mem_k = jnp.stack(mem_ks)
    new_mem_v = jnp.stack(mem_vs)
    return (y_p, y_s, conv_p, conv_s, ckv_p, kpe_p, ksc_p, ckv_s, kpe_s, ksc_s, new_mem_k, new_mem_v)
```

```python
import functools
import math

import jax
import jax.numpy as jnp
from jax import lax
from jax.experimental import pallas as pl
from jax.experimental.pallas import tpu as pltpu

EPS = 1e-6
ROPE_THETA = 10000.0
PAGE_SIZE = 128
N_HEADS = 8
QK_NOPE = 128
QK_ROPE = 64
QK_HEAD = QK_NOPE + QK_ROPE
V_HEAD = 128
MEM_HEADS = 4
MEM_HEAD_DIM = 128
D_MEM = MEM_HEADS * MEM_HEAD_DIM
CONV_WIDTH = 3
SM_SCALE = QK_HEAD ** -0.5
MEM_SCALE = MEM_HEAD_DIM ** -0.5

LANES = 128
SUBLANES = 8
HEAD_PAD = 2 * LANES
VMEM_LIMIT = 56 * 1024 * 1024
PAGES_PER_STEP = 8

_F32 = jnp.float32
_BF16 = jnp.bfloat16


def _dot(a, b):
    return jnp.dot(a, b, preferred_element_type=_F32)


def _dot_nt(a, b):
    return lax.dot_general(a, b, (((1,), (1,)), ((), ())), preferred_element_type=_F32)


def _rms(x, g):
    return x * lax.rsqrt(jnp.mean(x * x, axis=-1, keepdims=True) + EPS) * g


def _silu(z):
    return z / (1.0 + jnp.exp(-z))


def _const_spec(shape):
    nd = len(shape)
    return pl.BlockSpec(shape, lambda *_: (0,) * nd, pipeline_mode=pl.Buffered(1))


def _params(n_axes):
    return pltpu.CompilerParams(dimension_semantics=("arbitrary",) * n_axes,
                                vmem_limit_bytes=VMEM_LIMIT)


def _mem_attend(qm, k_ref, v_ref, bi, qg):
    outs = []
    for h in range(MEM_HEADS):
        sl = slice(h * MEM_HEAD_DIM, (h + 1) * MEM_HEAD_DIM)
        q = (_rms(qm[:, sl], qg) * MEM_SCALE).astype(_BF16)
        k = k_ref[bi, :, sl].astype(_BF16)
        v = v_ref[bi, :, sl].astype(_BF16)
        s = _dot_nt(q, k)
        p = jnp.exp(s - jnp.max(s, axis=-1, keepdims=True))
        o = _dot(p.astype(_BF16), v) / jnp.sum(p, axis=-1, keepdims=True)
        outs.append(o)
    return outs


def _mem_kv_kernel(mem_ref, ng_ref, w_ref, kg_ref, k_out, v_out):
    x = mem_ref[0]
    xn = _rms(x, ng_ref[0]).astype(_BF16)
    kv = _dot(xn, w_ref[0])
    kg = kg_ref[0]
    for h in range(MEM_HEADS):
        sl = slice(h * MEM_HEAD_DIM, (h + 1) * MEM_HEAD_DIM)
        k_out[0, 0, :, sl] = _rms(kv[:, sl], kg)
    v_out[0, 0] = kv[:, D_MEM:]


def _mem_kv(mem, norm_g, w_kv, k_g):
    b, m, d = mem.shape
    depth = norm_g.shape[0]
    out = jax.ShapeDtypeStruct((depth, b, m, D_MEM), _F32)
    return pl.pallas_call(
        _mem_kv_kernel,
        grid=(depth, b),
        in_specs=[
            pl.BlockSpec((1, m, d), lambda l, i: (i, 0, 0)),
            pl.BlockSpec((1, 1, d), lambda l, i: (l, 0, 0)),
            pl.BlockSpec((1, d, 2 * D_MEM), lambda l, i: (l, 0, 0)),
            pl.BlockSpec((1, 1, MEM_HEAD_DIM), lambda l, i: (l, 0, 0)),
        ],
        out_specs=[pl.BlockSpec((1, 1, m, D_MEM), lambda l, i: (l, i, 0, 0))] * 2,
        out_shape=[out, out],
        compiler_params=_params(2),
        name="mem_kv",
    )(mem, norm_g.reshape(depth, 1, d), w_kv.astype(_BF16), k_g.reshape(depth, 1, MEM_HEAD_DIM))


def _conv_layer_kernel(x_ref, hist_ref, mk_ref, mv_ref, ng_ref, win_ref, cw_ref, wout_ref, qg_ref,
                       y_ref, state_ref, cbuf, obuf, *, nb, tm, dc):
    rows = nb * tm
    d = x_ref.shape[-1]

    @pl.when(pl.program_id(1) == 0)
    def _():
        cbuf[:, SUBLANES - 2:SUBLANES, :] = hist_ref[...]

    x = x_ref[...].reshape(rows, d)
    xn = _rms(x, ng_ref[...]).astype(_BF16)
    proj = _dot(xn, win_ref[...])
    cu = proj[:, 2 * dc:3 * dc] * proj[:, :dc]
    cbuf[:, SUBLANES:SUBLANES + tm, :] = cu.reshape(nb, tm, dc)
    cw = cw_ref[...]
    conv = (cw[0:1] * cbuf[:, SUBLANES - 2:SUBLANES - 2 + tm, :].reshape(rows, dc)
            + cw[1:2] * cbuf[:, SUBLANES - 1:SUBLANES - 1 + tm, :].reshape(rows, dc)
            + cw[2:3] * cu)
    state = cbuf[:, tm + SUBLANES - 2:tm + SUBLANES, :]
    state_ref[...] = state
    cbuf[:, SUBLANES - 2:SUBLANES, :] = state
    zoff = 3 * dc + D_MEM
    obuf[:, :dc] = (proj[:, dc:2 * dc] * conv * _silu(proj[:, zoff:zoff + dc])).astype(_BF16)
    qg = qg_ref[...]
    for bi in range(nb):
        r = slice(bi * tm, (bi + 1) * tm)
        outs = _mem_attend(proj[r, 3 * dc:3 * dc + D_MEM], mk_ref, mv_ref, bi, qg)
        for h, o in enumerate(outs):
            c0 = dc + h * MEM_HEAD_DIM
            obuf[r, c0:c0 + MEM_HEAD_DIM] = (o * _silu(proj[r, zoff + c0:zoff + c0 + MEM_HEAD_DIM])).astype(_BF16)
    y_ref[...] = (x + _dot(obuf[...], wout_ref[...])).reshape(nb, tm, d)


def _conv_layer(x, hist, mem_k, mem_v, mem_layer, mem_q_g, norm_g, w_in, conv_w, w_out, *, nb, tm):
    b, t, d = x.shape
    dc = conv_w.shape[1]
    n_mem = mem_k.shape[1]
    mem_b = mem_k.shape[0] // 2 if mem_layer is not None else None
    off = mem_layer * (b // nb)
    kern = functools.partial(_conv_layer_kernel, nb=nb, tm=tm, dc=dc)
    return pl.pallas_call(
        kern,
        grid=(b // nb, t // tm),
        in_specs=[
            pl.BlockSpec((nb, tm, d), lambda i, j: (i, j, 0)),
            pl.BlockSpec((nb, CONV_WIDTH - 1, dc), lambda i, j: (i, 0, 0)),
            pl.BlockSpec((nb, n_mem, D_MEM), lambda i, j: (off + i, 0, 0)),
            pl.BlockSpec((nb, n_mem, D_MEM), lambda i, j: (off + i, 0, 0)),
            _const_spec((1, d)),
            _const_spec(w_in.shape),
            _const_spec(conv_w.shape),
            _const_spec(w_out.shape),
            _const_spec((1, MEM_HEAD_DIM)),
        ],
        out_specs=[
            pl.BlockSpec((nb, tm, d), lambda i, j: (i, j, 0)),
            pl.BlockSpec((nb, CONV_WIDTH - 1, dc), lambda i, j: (i, 0, 0)),
        ],
        out_shape=[jax.ShapeDtypeStruct((b, t, d), _F32),
                   jax.ShapeDtypeStruct((b, CONV_WIDTH - 1, dc), _F32)],
        scratch_shapes=[pltpu.VMEM((nb, tm + SUBLANES, dc), _F32),
                        pltpu.VMEM((nb * tm, dc + D_MEM), _BF16)],
        compiler_params=_params(2),
        name="conv_layer",
    )(x, hist, mem_k, mem_v, norm_g.reshape(1, d), w_in, conv_w, w_out, mem_q_g.reshape(1, MEM_HEAD_DIM))


def _rope_table(pos):
    inv = 1.0 / (ROPE_THETA ** (jnp.arange(0, QK_ROPE, 2, dtype=_F32) / QK_ROPE))
    ang = pos.astype(_F32)[:, None] * inv[None, :]
    c, s = jnp.cos(ang), jnp.sin(ang)
    return jnp.concatenate([c, c, -s, s], axis=-1)


def _swap_halves(w):
    half = w.shape[-1] // 2
    return jnp.concatenate([w[..., half:], w[..., :half]], axis=-1)


def _mla_weights(mla_w_in, mla_w_uq, mla_q_g, mla_k_g, q_lora, kv_lora):
    o = q_lora + kv_lora
    kpe_w = mla_w_in[:, o:o + QK_ROPE]
    w_in = jnp.concatenate([mla_w_in[:, :o], kpe_w, _swap_halves(kpe_w), mla_w_in[:, o + QK_ROPE:]], axis=1)
    wq = mla_w_uq.reshape(q_lora, N_HEADS, QK_HEAD)
    wq = jnp.concatenate([wq, _swap_halves(wq[..., QK_NOPE:])], axis=-1)
    w_uq = wq.reshape(q_lora, N_HEADS * HEAD_PAD)
    g_rope = mla_q_g[QK_NOPE:]
    q_gain = jnp.concatenate([mla_q_g[:QK_NOPE] * mla_k_g[:QK_NOPE], g_rope, _swap_halves(g_rope)])
    k_rope = mla_k_g[QK_NOPE:]
    k_gain = jnp.concatenate([k_rope, _swap_halves(k_rope)])
    return (w_in.astype(_BF16), w_uq.astype(_BF16), q_gain.reshape(1, HEAD_PAD), k_gain.reshape(1, LANES))


def _rope_lane_mask(rows):
    return lax.broadcasted_iota(jnp.int32, (rows, LANES), 1) < QK_ROPE


def _q_heads(cq, qlg, w_uq, q_gain, tab, rope_mask):
    q = _dot(_rms(cq, qlg).astype(_BF16), w_uq)
    heads = []
    for h in range(N_HEADS):
        a = q[:, h * HEAD_PAD:h * HEAD_PAD + LANES]
        b = q[:, h * HEAD_PAD + LANES:(h + 1) * HEAD_PAD]
        ss = (jnp.sum(a * a, axis=-1, keepdims=True)
              + jnp.sum(jnp.where(rope_mask, b * b, 0.0), axis=-1, keepdims=True))
        r = lax.rsqrt(ss / QK_HEAD + EPS)
        heads.append((a * r * q_gain[:, :LANES], b * r * q_gain[:, LANES:] * tab))
    return heads


def _kpe_rot(kpe_blk, k_gain, tab):
    kt = kpe_blk * k_gain * tab
    return kt + pltpu.roll(kt, QK_ROPE, axis=1)


def _head_select(cols, rows):
    lane = lax.broadcasted_iota(jnp.int32, (rows, N_HEADS), 1)
    out = jnp.zeros((rows, N_HEADS), _F32)
    for h, c in enumerate(cols):
        out = jnp.where(lane == h, c, out)
    return out


def _mla_prompt_kernel(x_ref, tab_ref, mk_ref, mv_ref, ng_ref, win_ref, qlg_ref, wuq_ref, qgain_ref,
                       ckvg_ref, wuk_ref, wuv_ref, kgain_ref, wout_ref, qg_ref,
                       y_ref, ckv_ref, kpe_ref, ksc_ref,
                       proj, qbuf, kbuf, vbuf, obuf, o3, m_ref, l_ref, acc_ref, *, tm, q_lora, kv_lora):
    i = pl.program_id(1)
    x = x_ref[0]
    tab = tab_ref[...]
    rope_mask = _rope_lane_mask(tm)
    xn = _rms(x, ng_ref[...]).astype(_BF16)
    proj[...] = _dot(xn, win_ref[...])

    heads = _q_heads(proj[:, :q_lora], qlg_ref[...], wuq_ref[...], qgain_ref[...] * SM_SCALE, tab, rope_mask)
    for h, (a, b) in enumerate(heads):
        qbuf[h, :, :LANES] = a.astype(_BF16)
        qbuf[h, :, LANES:] = b.astype(_BF16)

    ckv = _rms(proj[:, q_lora:q_lora + kv_lora], ckvg_ref[...])
    ckv_ref[0] = ckv
    ckv_b = ckv.astype(_BF16)
    k_nope = _dot(ckv_b, wuk_ref[...])
    v = _dot(ckv_b, wuv_ref[...])
    o = q_lora + kv_lora
    kpe_blk = proj[:, o:o + LANES]
    skpe = jnp.sum(jnp.where(rope_mask, kpe_blk * kpe_blk, 0.0), axis=-1, keepdims=True)
    rot = _kpe_rot(kpe_blk, kgain_ref[...], tab)
    kpe_ref[0] = rot[:, :QK_ROPE]
    row0 = pl.multiple_of(i * tm, tm)
    kscs = []
    for h in range(N_HEADS):
        kn = k_nope[:, h * QK_NOPE:(h + 1) * QK_NOPE]
        ksc = lax.rsqrt((jnp.sum(kn * kn, axis=-1, keepdims=True) + skpe) / QK_HEAD + EPS)
        kscs.append(ksc)
        kbuf[h, pl.ds(row0, tm), :LANES] = (kn * ksc).astype(_BF16)
        kbuf[h, pl.ds(row0, tm), LANES:] = (rot * ksc).astype(_BF16)
        vbuf[h, pl.ds(row0, tm), :] = v[:, h * V_HEAD:(h + 1) * V_HEAD].astype(_BF16)
    ksc_ref[0] = _head_select(kscs, tm)

    causal = (lax.broadcasted_iota(jnp.int32, (tm, tm), 0) >= lax.broadcasted_iota(jnp.int32, (tm, tm), 1))

    def attend_tile(h, start, mask):
        q = qbuf[h]
        s = _dot_nt(q, kbuf[h, pl.ds(start, tm), :])
        if mask:
            s = jnp.where(causal, s, -jnp.inf)
        m_old = m_ref[...]
        m_new = jnp.maximum(m_old, jnp.max(s, axis=-1, keepdims=True))
        alpha = jnp.exp(m_old - m_new)
        p = jnp.exp(s - m_new)
        l_ref[...] = alpha * l_ref[...] + jnp.sum(p, axis=-1, keepdims=True)
        acc_ref[...] = alpha * acc_ref[...] + _dot(p.astype(_BF16), vbuf[h, pl.ds(start, tm), :])
        m_ref[...] = m_new

    def head_body(h, carry):
        m_ref[...] = jnp.full(m_ref.shape, -jnp.inf, _F32)
        l_ref[...] = jnp.zeros(l_ref.shape, _F32)
        acc_ref[...] = jnp.zeros(acc_ref.shape, _F32)

        def kv_step(j, c):
            attend_tile(h, pl.multiple_of(j * tm, tm), False)
            return c

        lax.fori_loop(0, i, kv_step, 0)
        attend_tile(h, row0, True)
        o3[h] = acc_ref[...] / l_ref[...]
        return carry

    lax.fori_loop(0, N_HEADS, head_body, 0)

    d_mla = N_HEADS * V_HEAD
    zoff = o + LANES + D_MEM
    for h in range(N_HEADS):
        c0 = h * V_HEAD
        obuf[:, c0:c0 + V_HEAD] = (o3[h] * _silu(proj[:, zoff + c0:zoff + c0 + V_HEAD])).astype(_BF16)
    outs = _mem_attend(proj[:, o + LANES:o + LANES + D_MEM], mk_ref, mv_ref, 0, qg_ref[...])
    for h, om in enumerate(outs):
        c0 = d_mla + h * MEM_HEAD_DIM
        obuf[:, c0:c0 + MEM_HEAD_DIM] = (om * _silu(proj[:, zoff + c0:zoff + c0 + MEM_HEAD_DIM])).astype(_BF16)
    y_ref[0] = x + _dot(obuf[...], wout_ref[...])


def _mla_prompt(x, tab, mem_k, mem_v, mem_layer, mem_q_g, norm_g, w_in, q_lora_g, w_uq, q_gain, ckv_g,
                w_uk, w_uv, k_gain, w_out, *, tm):
    b, t, d = x.shape
    q_lora = q_lora_g.shape[0]
    kv_lora = ckv_g.shape[0]
    n_mem = mem_k.shape[1]
    off = mem_layer * b
    d_out = N_HEADS * V_HEAD + D_MEM
    kern = functools.partial(_mla_prompt_kernel, tm=tm, q_lora=q_lora, kv_lora=kv_lora)
    tok = lambda w: pl.BlockSpec((1, tm, w), lambda i, j: (i, j, 0))
    return pl.pallas_call(
        kern,
        grid=(b, t // tm),
        in_specs=[
            tok(d),
            pl.BlockSpec((tm, LANES), lambda i, j: (j, 0)),
            pl.BlockSpec((1, n_mem, D_MEM), lambda i, j: (off + i, 0, 0)),
            pl.BlockSpec((1, n_mem, D_MEM), lambda i, j: (off + i, 0, 0)),
            _const_spec((1, d)),
            _const_spec(w_in.shape),
            _const_spec((1, q_lora)),
            _const_spec(w_uq.shape),
            _const_spec((1, HEAD_PAD)),
            _const_spec((1, kv_lora)),
            _const_spec(w_uk.shape),
            _const_spec(w_uv.shape),
            _const_spec((1, LANES)),
            _const_spec(w_out.shape),
            _const_spec((1, MEM_HEAD_DIM)),
        ],
        out_specs=[tok(d), tok(kv_lora), tok(QK_ROPE), tok(N_HEADS)],
        out_shape=[jax.ShapeDtypeStruct((b, t, d), _F32),
                   jax.ShapeDtypeStruct((b, t, kv_lora), _F32),
                   jax.ShapeDtypeStruct((b, t, QK_ROPE), _F32),
                   jax.ShapeDtypeStruct((b, t, N_HEADS), _F32)],
        scratch_shapes=[
            pltpu.VMEM((tm, w_in.shape[1]), _F32),
            pltpu.VMEM((N_HEADS, tm, HEAD_PAD), _BF16),
            pltpu.VMEM((N_HEADS, t, HEAD_PAD), _BF16),
            pltpu.VMEM((N_HEADS, t, V_HEAD), _BF16),
            pltpu.VMEM((tm, d_out), _BF16),
            pltpu.VMEM((N_HEADS, tm, V_HEAD), _F32),
            pltpu.VMEM((tm, 1), _F32),
            pltpu.VMEM((tm, 1), _F32),
            pltpu.VMEM((tm, V_HEAD), _F32),
        ],
        compiler_params=_params(2),
        name="mla_prompt",
    )(x, tab, mem_k, mem_v, norm_g.reshape(1, d), w_in, q_lora_g.reshape(1, q_lora), w_uq, q_gain,
      ckv_g.reshape(1, kv_lora), w_uk, w_uv, k_gain, w_out, mem_q_g.reshape(1, MEM_HEAD_DIM))


def _mla_sample_proj_kernel(x_ref, tab_ref, ng_ref, win_ref, qlg_ref, wuq_ref, qgain_ref, ckvg_ref,
                            wuk_ref, wukt_ref, kgain_ref,
                            qabs_ref, qpe_ref, ckv_ref, kpe_ref, ksc_ref, rest_ref, *, q_lora, kv_lora):
    rows = x_ref.shape[0]
    tab = tab_ref[...]
    rope_mask = _rope_lane_mask(rows)
    xn = _rms(x_ref[...], ng_ref[...]).astype(_BF16)
    proj = _dot(xn, win_ref[...])
    o = q_lora + kv_lora
    rest_ref[...] = proj[:, o + LANES:]

    heads = _q_heads(proj[:, :q_lora], qlg_ref[...], wuq_ref[...], qgain_ref[...] * SM_SCALE, tab, rope_mask)
    for h, (a, b) in enumerate(heads):
        qabs_ref[:, h * kv_lora:(h + 1) * kv_lora] = _dot(a.astype(_BF16), wukt_ref[h]).astype(_BF16)
        qpe_ref[:, h * LANES:(h + 1) * LANES] = (b + pltpu.roll(b, QK_ROPE, axis=1)).astype(_BF16)

    ckv = _rms(proj[:, q_lora:o], ckvg_ref[...])
    ckv_ref[...] = ckv
    k_nope = _dot(ckv.astype(_BF16), wuk_ref[...])
    kpe_blk = proj[:, o:o + LANES]
    skpe = jnp.sum(jnp.where(rope_mask, kpe_blk * kpe_blk, 0.0), axis=-1, keepdims=True)
    kpe_ref[...] = _kpe_rot(kpe_blk, kgain_ref[...], tab)[:, :QK_ROPE]
    kscs = []
    for h in range(N_HEADS):
        kn = k_nope[:, h * QK_NOPE:(h + 1) * QK_NOPE]
        kscs.append(lax.rsqrt((jnp.sum(kn * kn, axis=-1, keepdims=True) + skpe) / QK_HEAD + EPS))
    ksc_ref[...] = _head_select(kscs, rows)


def _mla_sample_proj(x2, tab, norm_g, w_in, q_lora_g, w_uq, q_gain, ckv_g, w_uk, w_ukt, k_gain, *, tm):
    n, d = x2.shape
    q_lora = q_lora_g.shape[0]
    kv_lora = ckv_g.shape[0]
    n_rest = w_in.shape[1] - (q_lora + kv_lora + LANES)
    kern = functools.partial(_mla_sample_proj_kernel, q_lora=q_lora, kv_lora=kv_lora)
    tok = lambda w: pl.BlockSpec((tm, w), lambda i: (i, 0))
    return pl.pallas_call(
        kern,
        grid=(n // tm,),
        in_specs=[
            tok(d), tok(LANES),
            _const_spec((1, d)),
            _const_spec(w_in.shape),
            _const_spec((1, q_lora)),
            _const_spec(w_uq.shape),
            _const_spec((1, HEAD_PAD)),
            _const_spec((1, kv_lora)),
            _const_spec(w_uk.shape),
            _const_spec(w_ukt.shape),
            _const_spec((1, LANES)),
        ],
        out_specs=[tok(N_HEADS * kv_lora), tok(N_HEADS * LANES), tok(kv_lora), tok(QK_ROPE), tok(N_HEADS),
                   tok(n_rest)],
        out_shape=[jax.ShapeDtypeStruct((n, N_HEADS * kv_lora), _BF16),
                   jax.ShapeDtypeStruct((n, N_HEADS * LANES), _BF16),
                   jax.ShapeDtypeStruct((n, kv_lora), _F32),
                   jax.ShapeDtypeStruct((n, QK_ROPE), _F32),
                   jax.ShapeDtypeStruct((n, N_HEADS), _F32),
                   jax.ShapeDtypeStruct((n, n_rest), _F32)],
        compiler_params=_params(1),
        name="mla_sample_proj",
    )(x2, tab, norm_g.reshape(1, d), w_in, q_lora_g.reshape(1, q_lora), w_uq, q_gain,
      ckv_g.reshape(1, kv_lora), w_uk, w_ukt, k_gain)


def _mla_sample_attn_kernel(pt_ref, qabs_ref, qpe_ref, nckv_ref, nkpe_ref, nksc_ref, *rest, n_pages_step, dec):
    del pt_ref
    ckv_refs = rest[:n_pages_step]
    kpe_refs = rest[n_pages_step:2 * n_pages_step]
    ksc_refs = rest[2 * n_pages_step:3 * n_pages_step]
    o_ref, m_ref, l_ref, acc_ref = rest[3 * n_pages_step:]
    j = pl.program_id(1)
    rows = N_HEADS * dec
    qa = qabs_ref[0]
    qp = qpe_ref[0][:, :QK_ROPE]
    rr = lax.broadcasted_iota(jnp.int32, (rows, N_HEADS), 0)
    hh = lax.broadcasted_iota(jnp.int32, (rows, N_HEADS), 1)
    expand = (rr % N_HEADS == hh).astype(_F32)

    def scores(ckv, kpe, ksc):
        s = _dot_nt(qa, ckv.astype(_BF16)) + _dot_nt(qp, kpe.astype(_BF16))
        scale = lax.dot_general(expand, ksc, (((1,), (1,)), ((), ())), precision=lax.Precision.HIGHEST,
                                preferred_element_type=_F32)
        return s * scale

    @pl.when(j == 0)
    def _():
        ckv = nckv_ref[0]
        s = scores(ckv, nkpe_ref[0], nksc_ref[0])
        tq = lax.broadcasted_iota(jnp.int32, (rows, dec), 0) // N_HEADS
        tk = lax.broadcasted_iota(jnp.int32, (rows, dec), 1)
        s = jnp.where(tk <= tq, s, -jnp.inf)
        m = jnp.max(s, axis=-1, keepdims=True)
        p = jnp.exp(s - m)
        m_ref[...] = m
        l_ref[...] = jnp.sum(p, axis=-1, keepdims=True)
        acc_ref[...] = _dot(p.astype(_BF16), ckv.astype(_BF16))

    ckvs = [r[0].astype(_BF16) for r in ckv_refs]
    s = jnp.concatenate([scores(ckvs[p], kpe_refs[p][0], ksc_refs[p][0]) for p in range(n_pages_step)], axis=1)
    m_old = m_ref[...]
    m_new = jnp.maximum(m_old, jnp.max(s, axis=-1, keepdims=True))
    alpha = jnp.exp(m_old - m_new)
    p = jnp.exp(s - m_new).astype(_BF16)
    l_ref[...] = alpha * l_ref[...] + jnp.sum(p.astype(_F32), axis=-1, keepdims=True)
    acc = alpha * acc_ref[...]
    for pg in range(n_pages_step):
        acc = acc + _dot(p[:, pg * PAGE_SIZE:(pg + 1) * PAGE_SIZE], ckvs[pg])
    acc_ref[...] = acc
    m_ref[...] = m_new

    @pl.when(j == pl.num_programs(1) - 1)
    def _():
        o_ref[0] = acc_ref[...] / l_ref[...]


def _mla_sample_attn(page_table, qabs, qpe, nckv, nkpe, nksc, cache_ckv, cache_kpe, cache_ksc):
    b, rows, kv_lora = qabs.shape
    dec = nckv.shape[1]
    n_pages = page_table.shape[1]
    pps = math.gcd(n_pages, PAGES_PER_STEP)
    kern = functools.partial(_mla_sample_attn_kernel, n_pages_step=pps, dec=dec)

    def page_spec(width, p):
        return pl.BlockSpec((1, PAGE_SIZE, width), lambda i, j, pt: (pt[i, j * pps + p], 0, 0))

    per_b = lambda shape: pl.BlockSpec((1,) + shape, lambda i, j, pt: (i, 0, 0))
    grid_spec = pltpu.PrefetchScalarGridSpec(
        num_scalar_prefetch=1,
        grid=(b, n_pages // pps),
        in_specs=([per_b((rows, kv_lora)), per_b((rows, LANES)), per_b((dec, kv_lora)), per_b((dec, QK_ROPE)),
                   per_b((dec, N_HEADS))]
                  + [page_spec(kv_lora, p) for p in range(pps)]
                  + [page_spec(QK_ROPE, p) for p in range(pps)]
                  + [page_spec(N_HEADS, p) for p in range(pps)]),
        out_specs=per_b((rows, kv_lora)),
        scratch_shapes=[pltpu.VMEM((rows, 1), _F32), pltpu.VMEM((rows, 1), _F32),
                        pltpu.VMEM((rows, kv_lora), _F32)],
    )
    return pl.pallas_call(
        kern,
        grid_spec=grid_spec,
        out_shape=jax.ShapeDtypeStruct((b, rows, kv_lora), _F32),
        compiler_params=_params(2),
        name="mla_sample_attn",
    )(page_table, qabs, qpe, nckv, nkpe, nksc, *([cache_ckv] * pps), *([cache_kpe] * pps), *([cache_ksc] * pps))


def _mla_sample_out_kernel(x_ref, olat_ref, rest_ref, mk_ref, mv_ref, wuv_ref, wout_ref, qg_ref,
                           y_ref, obuf, *, nb, dec, kv_lora):
    d_mla = N_HEADS * V_HEAD
    zoff = D_MEM
    for h in range(N_HEADS):
        c0 = h * V_HEAD
        y_h = _dot(olat_ref[:, h * kv_lora:(h + 1) * kv_lora].astype(_BF16), wuv_ref[h])
        obuf[:, c0:c0 + V_HEAD] = (y_h * _silu(rest_ref[:, zoff + c0:zoff + c0 + V_HEAD])).astype(_BF16)
    qg = qg_ref[...]
    for bi in range(nb):
        r = slice(bi * dec, (bi + 1) * dec)
        outs = _mem_attend(rest_ref[r, :D_MEM], mk_ref, mv_ref, bi, qg)
        for h, om in enumerate(outs):
            c0 = d_mla + h * MEM_HEAD_DIM
            obuf[r, c0:c0 + MEM_HEAD_DIM] = (om * _silu(rest_ref[r, zoff + c0:zoff + c0 + MEM_HEAD_DIM])).astype(_BF16)
    y_ref[...] = x_ref[...] + _dot(obuf[...], wout_ref[...])


def _mla_sample_out(x2, olat, rest, mem_k, mem_v, mem_layer, mem_q_g, w_uv3, w_out, *, nb, dec):
    n, d = x2.shape
    kv_lora = w_uv3.shape[1]
    n_mem = mem_k.shape[1]
    rows = nb * dec
    off = mem_layer * (n // rows)
    kern = functools.partial(_mla_sample_out_kernel, nb=nb, dec=dec, kv_lora=kv_lora)
    tok = lambda w: pl.BlockSpec((rows, w), lambda i: (i, 0))
    return pl.pallas_call(
        kern,
        grid=(n // rows,),
        in_specs=[
            tok(d), tok(olat.shape[1]), tok(rest.shape[1]),
            pl.BlockSpec((nb, n_mem, D_MEM), lambda i: (off + i, 0, 0)),
            pl.BlockSpec((nb, n_mem, D_MEM), lambda i: (off + i, 0, 0)),
            _const_spec(w_uv3.shape),
            _const_spec(w_out.shape),
            _const_spec((1, MEM_HEAD_DIM)),
        ],
        out_specs=tok(d),
        out_shape=jax.ShapeDtypeStruct((n, d), _F32),
        scratch_shapes=[pltpu.VMEM((rows, w_out.shape[0]), _BF16)],
        compiler_params=_params(1),
        name="mla_sample_out",
    )(x2, olat, rest, mem_k, mem_v, w_uv3, w_out, mem_q_g.reshape(1, MEM_HEAD_DIM))


def _pick_tile(n, target):
    t = min(n, target)
    while n % t or t % SUBLANES:
        t -= 1
    return t


def kernel(x_prompt, x_sample, mem_prompt, state_conv, cache_ckv, cache_kpe, cache_kscale, cache_mem_k, cache_mem_v, page_table, conv_norm_g, conv_w_in, conv_w, conv_w_out, mla_norm_g, mla_w_in, mla_q_lora_g, mla_w_uq, mla_ckv_g, mla_w_uk, mla_w_uv, mla_q_g, mla_k_g, mla_w_out, mem_norm_g, mem_w_kv, mem_q_g, mem_k_g):
    bp, seq, d = x_prompt.shape
    bs, dec, _ = x_sample.shape
    depth = mem_norm_g.shape[0]
    n_mem = mem_prompt.shape[1]
    dc = conv_w.shape[1]
    q_lora = mla_q_lora_g.shape[0]
    kv_lora = mla_ckv_g.shape[0]
    past = page_table.shape[1] * PAGE_SIZE

    tm_p = _pick_tile(seq, 512)
    nb_s = _pick_tile(bs, 8) if bs % 8 == 0 else bs

    mem_k, mem_v = _mem_kv(mem_prompt, mem_norm_g, mem_w_kv, mem_k_g)
    mem_k2 = mem_k.reshape(depth * bp, n_mem, D_MEM)
    mem_v2 = mem_v.reshape(depth * bp, n_mem, D_MEM)
    smem_k2 = cache_mem_k.reshape(depth * bs, n_mem, D_MEM)
    smem_v2 = cache_mem_v.reshape(depth * bs, n_mem, D_MEM)

    cw_in = conv_w_in.astype(_BF16)
    cw_out = conv_w_out.astype(_BF16)
    hist_p = jnp.zeros((bp, CONV_WIDTH - 1, dc), _F32)
    y_p, conv_p = _conv_layer(x_prompt, hist_p, mem_k2, mem_v2, 0, mem_q_g[0], conv_norm_g, cw_in, conv_w, cw_out,
                              nb=1, tm=tm_p)
    y_s, conv_s = _conv_layer(x_sample, state_conv, smem_k2, smem_v2, 0, mem_q_g[0], conv_norm_g, cw_in, conv_w,
                              cw_out, nb=nb_s, tm=dec)

    w_in, w_uq, q_gain, k_gain = _mla_weights(mla_w_in, mla_w_uq, mla_q_g, mla_k_g, q_lora, kv_lora)
    w_uk2 = mla_w_uk.reshape(kv_lora, N_HEADS * QK_NOPE).astype(_BF16)
    w_uv2 = mla_w_uv.reshape(kv_lora, N_HEADS * V_HEAD).astype(_BF16)
    w_out = mla_w_out.astype(_BF16)
    tab_p = _rope_table(jnp.arange(seq, dtype=jnp.int32))
    y_p, ckv_p, kpe_p, ksc_p = _mla_prompt(y_p, tab_p, mem_k2, mem_v2, 1, mem_q_g[1], mla_norm_g, w_in,
                                           mla_q_lora_g, w_uq, q_gain, mla_ckv_g, w_uk2, w_uv2, k_gain, w_out,
                                           tm=tm_p)

    n_s = bs * dec
    tab_s = jnp.tile(_rope_table(past + jnp.arange(dec, dtype=jnp.int32)), (bs, 1))
    w_ukt = jnp.transpose(mla_w_uk, (1, 2, 0)).astype(_BF16)
    w_uv3 = jnp.transpose(mla_w_uv, (1, 0, 2)).astype(_BF16)
    qabs, qpe, ckv_s, kpe_s, ksc_s, rest = _mla_sample_proj(
        y_s.reshape(n_s, d), tab_s, mla_norm_g, w_in, mla_q_lora_g, w_uq, q_gain, mla_ckv_g, w_uk2, w_ukt, k_gain,
        tm=_pick_tile(n_s, 512))
    rows = dec * N_HEADS
    olat = _mla_sample_attn(page_table, qabs.reshape(bs, rows, kv_lora), qpe.reshape(bs, rows, LANES),
                            ckv_s.reshape(bs, dec, kv_lora), kpe_s.reshape(bs, dec, QK_ROPE),
                            ksc_s.reshape(bs, dec, N_HEADS), cache_ckv, cache_kpe, cache_kscale)
    y_s2 = _mla_sample_out(y_s.reshape(n_s, d), olat.reshape(n_s, N_HEADS * kv_lora), rest, smem_k2, smem_v2, 1,
                           mem_q_g[1], w_uv3, w_out, nb=nb_s, dec=dec)

    new_shape = (depth, bp, n_mem, MEM_HEADS, MEM_HEAD_DIM)
    return (y_p, y_s2.reshape(bs, dec, d), conv_p, conv_s, ckv_p, kpe_p, ksc_p,
            ckv_s.reshape(bs, dec, kv_lora), kpe_s.reshape(bs, dec, QK_ROPE), ksc_s.reshape(bs, dec, N_HEADS),
            mem_k.reshape(new_shape), mem_v.reshape(new_shape))
```

```python
import functools
import math

import jax
import jax.numpy as jnp
from jax import lax
from jax.experimental import pallas as pl
from jax.experimental.pallas import tpu as pltpu

EPS = 1e-6
ROPE_THETA = 10000.0
PAGE_SIZE = 128
N_HEADS = 8
QK_NOPE = 128
QK_ROPE = 64
QK_HEAD = QK_NOPE + QK_ROPE
V_HEAD = 128
MEM_HEADS = 4
MEM_HEAD_DIM = 128
D_MEM = MEM_HEADS * MEM_HEAD_DIM
CONV_WIDTH = 3
SM_SCALE = QK_HEAD ** -0.5
MEM_SCALE = MEM_HEAD_DIM ** -0.5

LANES = 128
SUBLANES = 8
HEAD_PAD = 2 * LANES
VMEM_LIMIT = 60000 * 1024
PAGES_PER_STEP = 16
SAMPLE_CHAINS = 2
ATTN_Q_ROWS = 256
LOG2E = math.log2(math.e)

_F32 = jnp.float32
_BF16 = jnp.bfloat16


def _dot(a, b):
    return jnp.dot(a, b, preferred_element_type=_F32)


def _dot_nt(a, b):
    return lax.dot_general(a, b, (((1,), (1,)), ((), ())), preferred_element_type=_F32)


def _rms(x, g):
    return x * lax.rsqrt(jnp.mean(x * x, axis=-1, keepdims=True) + EPS) * g


def _silu(z):
    return z / (1.0 + jnp.exp(-z))


def _const_spec(shape):
    nd = len(shape)
    return pl.BlockSpec(shape, lambda *_: (0,) * nd, pipeline_mode=pl.Buffered(1))


def _params(n_axes):
    return pltpu.CompilerParams(dimension_semantics=("arbitrary",) * n_axes,
                                vmem_limit_bytes=VMEM_LIMIT)


def _mem_attend(qm, k_ref, v_ref, bi, qg):
    n_mem = k_ref.shape[1] // MEM_HEADS
    outs = []
    for h in range(MEM_HEADS):
        sl = slice(h * MEM_HEAD_DIM, (h + 1) * MEM_HEAD_DIM)
        q = (_rms(qm[:, sl], qg) * (MEM_SCALE * LOG2E)).astype(_BF16)
        k = k_ref[bi, pl.ds(h, n_mem, stride=MEM_HEADS), :].astype(_BF16)
        v = v_ref[bi, pl.ds(h, n_mem, stride=MEM_HEADS), :].astype(_BF16)
        s = _dot_nt(q, k)
        p = jnp.exp2(s - jnp.max(s, axis=-1, keepdims=True))
        o = _dot(p.astype(_BF16), v) / jnp.sum(p, axis=-1, keepdims=True)
        outs.append(o)
    return outs


def _mem_kv_kernel(mem_ref, ng_ref, w_ref, kg_ref, k_out, v_out):
    x = mem_ref[0]
    xn = _rms(x, ng_ref[0]).astype(_BF16)
    kv = _dot(xn, w_ref[0])
    kg = kg_ref[0]
    m = x.shape[0]
    for h in range(MEM_HEADS):
        sl = slice(h * MEM_HEAD_DIM, (h + 1) * MEM_HEAD_DIM)
        rows = pl.ds(h, m, stride=MEM_HEADS)
        k_out[0, 0, rows, :] = _rms(kv[:, sl], kg)
        v_out[0, 0, rows, :] = kv[:, D_MEM + h * MEM_HEAD_DIM:D_MEM + (h + 1) * MEM_HEAD_DIM]


def _mem_kv(mem, norm_g, w_kv, k_g):
    b, m, d = mem.shape
    depth = norm_g.shape[0]
    out = jax.ShapeDtypeStruct((depth, b, m * MEM_HEADS, MEM_HEAD_DIM), _F32)
    return pl.pallas_call(
        _mem_kv_kernel,
        grid=(depth, b),
        in_specs=[
            pl.BlockSpec((1, m, d), lambda l, i: (i, 0, 0)),
            pl.BlockSpec((1, 1, d), lambda l, i: (l, 0, 0)),
            pl.BlockSpec((1, d, 2 * D_MEM), lambda l, i: (l, 0, 0)),
            pl.BlockSpec((1, 1, MEM_HEAD_DIM), lambda l, i: (l, 0, 0)),
        ],
        out_specs=[pl.BlockSpec((1, 1, m * MEM_HEADS, MEM_HEAD_DIM), lambda l, i: (l, i, 0, 0))] * 2,
        out_shape=[out, out],
        compiler_params=_params(2),
        name="mem_kv",
    )(mem, norm_g.reshape(depth, 1, d), w_kv.astype(_BF16), k_g.reshape(depth, 1, MEM_HEAD_DIM))


def _conv_layer_kernel(x_ref, hist_ref, mk_ref, mv_ref, ng_ref, win_ref, cw_ref, wout_ref, qg_ref,
                       y_ref, state_ref, cbuf, obuf, *, nb, tm, dc):
    rows = nb * tm
    d = x_ref.shape[-1]

    @pl.when(pl.program_id(1) == 0)
    def _():
        cbuf[:, SUBLANES - 2:SUBLANES, :] = hist_ref[...]

    x = x_ref[...].reshape(rows, d)
    xn = _rms(x, ng_ref[...]).astype(_BF16)
    proj = _dot(xn, win_ref[...])
    cu = proj[:, 2 * dc:3 * dc] * proj[:, :dc]
    cbuf[:, SUBLANES:SUBLANES + tm, :] = cu.reshape(nb, tm, dc)
    cw = cw_ref[...]
    conv = (cw[0:1] * cbuf[:, SUBLANES - 2:SUBLANES - 2 + tm, :].reshape(rows, dc)
            + cw[1:2] * cbuf[:, SUBLANES - 1:SUBLANES - 1 + tm, :].reshape(rows, dc)
            + cw[2:3] * cu)
    state = cbuf[:, tm + SUBLANES - 2:tm + SUBLANES, :]
    state_ref[...] = state
    cbuf[:, SUBLANES - 2:SUBLANES, :] = state
    zoff = 3 * dc + D_MEM
    obuf[:, :dc] = (proj[:, dc:2 * dc] * conv * _silu(proj[:, zoff:zoff + dc])).astype(_BF16)
    qg = qg_ref[...]
    for bi in range(nb):
        r = slice(bi * tm, (bi + 1) * tm)
        outs = _mem_attend(proj[r, 3 * dc:3 * dc + D_MEM], mk_ref, mv_ref, bi, qg)
        for h, o in enumerate(outs):
            c0 = dc + h * MEM_HEAD_DIM
            obuf[r, c0:c0 + MEM_HEAD_DIM] = (o * _silu(proj[r, zoff + c0:zoff + c0 + MEM_HEAD_DIM])).astype(_BF16)
    y_ref[...] = (x + _dot(obuf[...], wout_ref[...])).reshape(nb, tm, d)


def _conv_layer(x, hist, mem_k, mem_v, mem_layer, mem_q_g, norm_g, w_in, conv_w, w_out, *, nb, tm):
    b, t, d = x.shape
    dc = conv_w.shape[1]
    mem_rows = mem_k.shape[1]
    off = mem_layer * (b // nb)
    kern = functools.partial(_conv_layer_kernel, nb=nb, tm=tm, dc=dc)
    return pl.pallas_call(
        kern,
        grid=(b // nb, t // tm),
        in_specs=[
            pl.BlockSpec((nb, tm, d), lambda i, j: (i, j, 0)),
            pl.BlockSpec((nb, CONV_WIDTH - 1, dc), lambda i, j: (i, 0, 0)),
            pl.BlockSpec((nb, mem_rows, MEM_HEAD_DIM), lambda i, j: (off + i, 0, 0)),
            pl.BlockSpec((nb, mem_rows, MEM_HEAD_DIM), lambda i, j: (off + i, 0, 0)),
            _const_spec((1, d)),
            _const_spec(w_in.shape),
            _const_spec(conv_w.shape),
            _const_spec(w_out.shape),
            _const_spec((1, MEM_HEAD_DIM)),
        ],
        out_specs=[
            pl.BlockSpec((nb, tm, d), lambda i, j: (i, j, 0)),
            pl.BlockSpec((nb, CONV_WIDTH - 1, dc), lambda i, j: (i, 0, 0)),
        ],
        out_shape=[jax.ShapeDtypeStruct((b, t, d), _F32),
                   jax.ShapeDtypeStruct((b, CONV_WIDTH - 1, dc), _F32)],
        scratch_shapes=[pltpu.VMEM((nb, tm + SUBLANES, dc), _F32),
                        pltpu.VMEM((nb * tm, dc + D_MEM), _BF16)],
        compiler_params=_params(2),
        name="conv_layer",
    )(x, hist, mem_k, mem_v, norm_g.reshape(1, d), w_in, conv_w, w_out, mem_q_g.reshape(1, MEM_HEAD_DIM))


def _rope_table(pos):
    inv = 1.0 / (ROPE_THETA ** (jnp.arange(0, QK_ROPE, 2, dtype=_F32) / QK_ROPE))
    ang = pos.astype(_F32)[:, None] * inv[None, :]
    c, s = jnp.cos(ang), jnp.sin(ang)
    return jnp.concatenate([c, c, -s, s], axis=-1)


def _swap_halves(w):
    half = w.shape[-1] // 2
    return jnp.concatenate([w[..., half:], w[..., :half]], axis=-1)


def _mla_weights(mla_w_in, mla_w_uq, mla_q_g, mla_k_g, q_lora, kv_lora):
    o = q_lora + kv_lora
    kpe_w = mla_w_in[:, o:o + QK_ROPE]
    w_in = jnp.concatenate([mla_w_in[:, :o], kpe_w, _swap_halves(kpe_w), mla_w_in[:, o + QK_ROPE:]], axis=1)
    wq = mla_w_uq.reshape(q_lora, N_HEADS, QK_HEAD)
    wq = jnp.concatenate([wq, _swap_halves(wq[..., QK_NOPE:])], axis=-1)
    w_uq = wq.reshape(q_lora, N_HEADS * HEAD_PAD)
    g_rope = mla_q_g[QK_NOPE:]
    q_gain = jnp.concatenate([mla_q_g[:QK_NOPE] * mla_k_g[:QK_NOPE], g_rope, _swap_halves(g_rope)])
    k_rope = mla_k_g[QK_NOPE:]
    k_gain = jnp.concatenate([k_rope, _swap_halves(k_rope)])
    return (w_in.astype(_BF16), w_uq.astype(_BF16), q_gain.reshape(1, HEAD_PAD), k_gain.reshape(1, LANES))


def _rope_lane_mask(rows):
    return lax.broadcasted_iota(jnp.int32, (rows, LANES), 1) < QK_ROPE


def _q_heads(cq, qlg, wuq_ref, q_gain, tab, rope_mask):
    cqn = _rms(cq, qlg).astype(_BF16)
    for h in range(N_HEADS):
        q = _dot(cqn, wuq_ref[:, h * HEAD_PAD:(h + 1) * HEAD_PAD])
        a = q[:, :LANES]
        b = q[:, LANES:]
        ss = (jnp.sum(a * a, axis=-1, keepdims=True)
              + jnp.sum(jnp.where(rope_mask, b * b, 0.0), axis=-1, keepdims=True))
        r = lax.rsqrt(ss / QK_HEAD + EPS)
        yield a * r * q_gain[:, :LANES], b * r * q_gain[:, LANES:] * tab


def _kpe_rot(kpe_blk, k_gain, tab):
    kt = kpe_blk * k_gain * tab
    return kt + pltpu.roll(kt, QK_ROPE, axis=1)


def _head_select(cols, rows):
    lane = lax.broadcasted_iota(jnp.int32, (rows, N_HEADS), 1)
    out = jnp.zeros((rows, N_HEADS), _F32)
    for h, c in enumerate(cols):
        out = jnp.where(lane == h, c, out)
    return out


def _mla_prompt_kernel(x_ref, tab_ref, mk_ref, mv_ref, ng_ref, win_ref, qlg_ref, wuq_ref, qgain_ref,
                       ckvg_ref, wuk_ref, wuv_ref, kgain_ref, wout_ref, qg_ref,
                       y_ref, ckv_ref, kpe_ref, ksc_ref,
                       proj, qbuf, kbuf, vbuf, obuf, o3, *, tm, tq, q_lora, kv_lora):
    i = pl.program_id(1)
    x = x_ref[0]
    tab = tab_ref[...]
    rope_mask = _rope_lane_mask(tm)
    xn = _rms(x, ng_ref[...]).astype(_BF16)
    proj[...] = _dot(xn, win_ref[...])

    heads = _q_heads(proj[:, :q_lora], qlg_ref[...], wuq_ref, qgain_ref[...] * (SM_SCALE * LOG2E), tab, rope_mask)
    for h, (a, b) in enumerate(heads):
        qbuf[h, :, :LANES] = a.astype(_BF16)
        qbuf[h, :, LANES:] = b.astype(_BF16)

    ckv = _rms(proj[:, q_lora:q_lora + kv_lora], ckvg_ref[...])
    ckv_ref[0] = ckv
    ckv_b = ckv.astype(_BF16)
    o = q_lora + kv_lora
    kpe_blk = proj[:, o:o + LANES]
    skpe = jnp.sum(jnp.where(rope_mask, kpe_blk * kpe_blk, 0.0), axis=-1, keepdims=True)
    rot = _kpe_rot(kpe_blk, kgain_ref[...], tab)
    kpe_ref[0] = rot[:, :QK_ROPE]
    row0 = pl.multiple_of(i * tm, tm)
    kscs = []
    for h2 in range(0, N_HEADS, 2):
        k_pair = _dot(ckv_b, wuk_ref[:, h2 * QK_NOPE:(h2 + 2) * QK_NOPE])
        v_pair = _dot(ckv_b, wuv_ref[:, h2 * V_HEAD:(h2 + 2) * V_HEAD])
        for h in (h2, h2 + 1):
            kn = k_pair[:, (h - h2) * QK_NOPE:(h - h2 + 1) * QK_NOPE]
            ksc = lax.rsqrt((jnp.sum(kn * kn, axis=-1, keepdims=True) + skpe) / QK_HEAD + EPS)
            kscs.append(ksc)
            kbuf[h, pl.ds(row0, tm), :LANES] = (kn * ksc).astype(_BF16)
            kbuf[h, pl.ds(row0, tm), LANES:] = (rot * ksc).astype(_BF16)
            vbuf[h, pl.ds(row0, tm), :V_HEAD] = v_pair[:, (h - h2) * V_HEAD:(h - h2 + 1) * V_HEAD].astype(_BF16)
            vbuf[h, pl.ds(row0, tm), V_HEAD:] = jnp.ones((tm, LANES), _BF16)
    ksc_ref[0] = _head_select(kscs, tm)

    causal = (lax.broadcasted_iota(jnp.int32, (tq, tq), 0) >= lax.broadcasted_iota(jnp.int32, (tq, tq), 1))

    def attend_block(h, r0, n_keys):
        q = qbuf[h, r0:r0 + tq, :]
        s = _dot_nt(q, kbuf[h, 0:n_keys, :])
        diag = jnp.where(causal, s[:, n_keys - tq:], -jnp.inf)
        m = jnp.max(diag, axis=-1, keepdims=True)
        if n_keys > tq:
            past = s[:, :n_keys - tq]
            m = jnp.maximum(m, jnp.max(past, axis=-1, keepdims=True))
            p = jnp.concatenate([jnp.exp2(past - m), jnp.exp2(diag - m)], axis=1)
        else:
            p = jnp.exp2(diag - m)
        ov = _dot(p.astype(_BF16), vbuf[h, 0:n_keys, :])
        o3[h, r0:r0 + tq, :] = ov[:, :V_HEAD] / ov[:, V_HEAD:]

    for ti in range(kbuf.shape[1] // tm):
        @pl.when(i == ti)
        def _(ti=ti):
            def head_body(hp, carry):
                for r0 in range(0, tm, tq):
                    for dh in range(2):
                        attend_block(2 * hp + dh, r0, ti * tm + r0 + tq)
                return carry

            lax.fori_loop(0, N_HEADS // 2, head_body, 0)

    d_mla = N_HEADS * V_HEAD
    zoff = o + LANES + D_MEM
    for h in range(N_HEADS):
        c0 = h * V_HEAD
        obuf[:, c0:c0 + V_HEAD] = (o3[h] * _silu(proj[:, zoff + c0:zoff + c0 + V_HEAD])).astype(_BF16)
    outs = _mem_attend(proj[:, o + LANES:o + LANES + D_MEM], mk_ref, mv_ref, 0, qg_ref[...])
    for h, om in enumerate(outs):
        c0 = d_mla + h * MEM_HEAD_DIM
        obuf[:, c0:c0 + MEM_HEAD_DIM] = (om * _silu(proj[:, zoff + c0:zoff + c0 + MEM_HEAD_DIM])).astype(_BF16)
    y_ref[0] = x + _dot(obuf[...], wout_ref[...])


def _mla_prompt(x, tab, mem_k, mem_v, mem_layer, mem_q_g, norm_g, w_in, q_lora_g, w_uq, q_gain, ckv_g,
                w_uk, w_uv, k_gain, w_out, *, tm):
    b, t, d = x.shape
    q_lora = q_lora_g.shape[0]
    kv_lora = ckv_g.shape[0]
    mem_rows = mem_k.shape[1]
    off = mem_layer * b
    d_out = N_HEADS * V_HEAD + D_MEM
    tq = math.gcd(tm, ATTN_Q_ROWS)
    kern = functools.partial(_mla_prompt_kernel, tm=tm, tq=tq, q_lora=q_lora, kv_lora=kv_lora)
    tok = lambda w: pl.BlockSpec((1, tm, w), lambda i, j: (i, j, 0))
    return pl.pallas_call(
        kern,
        grid=(b, t // tm),
        in_specs=[
            tok(d),
            pl.BlockSpec((tm, LANES), lambda i, j: (j, 0)),
            pl.BlockSpec((1, mem_rows, MEM_HEAD_DIM), lambda i, j: (off + i, 0, 0)),
            pl.BlockSpec((1, mem_rows, MEM_HEAD_DIM), lambda i, j: (off + i, 0, 0)),
            _const_spec((1, d)),
            _const_spec(w_in.shape),
            _const_spec((1, q_lora)),
            _const_spec(w_uq.shape),
            _const_spec((1, HEAD_PAD)),
            _const_spec((1, kv_lora)),
            _const_spec(w_uk.shape),
            _const_spec(w_uv.shape),
            _const_spec((1, LANES)),
            _const_spec(w_out.shape),
            _const_spec((1, MEM_HEAD_DIM)),
        ],
        out_specs=[tok(d), tok(kv_lora), tok(QK_ROPE), tok(N_HEADS)],
        out_shape=[jax.ShapeDtypeStruct((b, t, d), _F32),
                   jax.ShapeDtypeStruct((b, t, kv_lora), _F32),
                   jax.ShapeDtypeStruct((b, t, QK_ROPE), _F32),
                   jax.ShapeDtypeStruct((b, t, N_HEADS), _F32)],
        scratch_shapes=[
            pltpu.VMEM((tm, w_in.shape[1]), _F32),
            pltpu.VMEM((N_HEADS, tm, HEAD_PAD), _BF16),
            pltpu.VMEM((N_HEADS, t, HEAD_PAD), _BF16),
            pltpu.VMEM((N_HEADS, t, V_HEAD + LANES), _BF16),
            pltpu.VMEM((tm, d_out), _BF16),
            pltpu.VMEM((N_HEADS, tm, V_HEAD), _F32),
        ],
        compiler_params=_params(2),
        name="mla_prompt",
    )(x, tab, mem_k, mem_v, norm_g.reshape(1, d), w_in, q_lora_g.reshape(1, q_lora), w_uq, q_gain,
      ckv_g.reshape(1, kv_lora), w_uk, w_uv, k_gain, w_out, mem_q_g.reshape(1, MEM_HEAD_DIM))


def _mla_sample_proj_kernel(x_ref, tab_ref, ng_ref, win_ref, qlg_ref, wuq_ref, qgain_ref, ckvg_ref,
                            wuk_ref, wukt_ref, kgain_ref,
                            qabs_ref, qpe_ref, ckv_ref, kpe_ref, ksc_ref, rest_ref, *, q_lora, kv_lora):
    rows = x_ref.shape[0]
    tab = tab_ref[...]
    rope_mask = _rope_lane_mask(rows)
    xn = _rms(x_ref[...], ng_ref[...]).astype(_BF16)
    proj = _dot(xn, win_ref[...])
    o = q_lora + kv_lora
    rest_ref[...] = proj[:, o + LANES:]

    heads = _q_heads(proj[:, :q_lora], qlg_ref[...], wuq_ref, qgain_ref[...] * (SM_SCALE * LOG2E), tab, rope_mask)
    for h, (a, b) in enumerate(heads):
        qabs_ref[:, h * kv_lora:(h + 1) * kv_lora] = _dot(a.astype(_BF16), wukt_ref[h]).astype(_BF16)
        qpe_ref[:, h * LANES:(h + 1) * LANES] = (b + pltpu.roll(b, QK_ROPE, axis=1)).astype(_BF16)

    ckv = _rms(proj[:, q_lora:o], ckvg_ref[...])
    ckv_ref[...] = ckv
    k_nope = _dot(ckv.astype(_BF16), wuk_ref[...])
    kpe_blk = proj[:, o:o + LANES]
    skpe = jnp.sum(jnp.where(rope_mask, kpe_blk * kpe_blk, 0.0), axis=-1, keepdims=True)
    kpe_ref[...] = _kpe_rot(kpe_blk, kgain_ref[...], tab)[:, :QK_ROPE]
    kscs = []
    for h in range(N_HEADS):
        kn = k_nope[:, h * QK_NOPE:(h + 1) * QK_NOPE]
        kscs.append(lax.rsqrt((jnp.sum(kn * kn, axis=-1, keepdims=True) + skpe) / QK_HEAD + EPS))
    ksc_ref[...] = _head_select(kscs, rows)


def _mla_sample_proj(x2, tab, norm_g, w_in, q_lora_g, w_uq, q_gain, ckv_g, w_uk, w_ukt, k_gain, *, tm):
    n, d = x2.shape
    q_lora = q_lora_g.shape[0]
    kv_lora = ckv_g.shape[0]
    n_rest = w_in.shape[1] - (q_lora + kv_lora + LANES)
    kern = functools.partial(_mla_sample_proj_kernel, q_lora=q_lora, kv_lora=kv_lora)
    tok = lambda w: pl.BlockSpec((tm, w), lambda i: (i, 0))
    return pl.pallas_call(
        kern,
        grid=(n // tm,),
        in_specs=[
            tok(d), tok(LANES),
            _const_spec((1, d)),
            _const_spec(w_in.shape),
            _const_spec((1, q_lora)),
            _const_spec(w_uq.shape),
            _const_spec((1, HEAD_PAD)),
            _const_spec((1, kv_lora)),
            _const_spec(w_uk.shape),
            _const_spec(w_ukt.shape),
            _const_spec((1, LANES)),
        ],
        out_specs=[tok(N_HEADS * kv_lora), tok(N_HEADS * LANES), tok(kv_lora), tok(QK_ROPE), tok(N_HEADS),
                   tok(n_rest)],
        out_shape=[jax.ShapeDtypeStruct((n, N_HEADS * kv_lora), _BF16),
                   jax.ShapeDtypeStruct((n, N_HEADS * LANES), _BF16),
                   jax.ShapeDtypeStruct((n, kv_lora), _F32),
                   jax.ShapeDtypeStruct((n, QK_ROPE), _F32),
                   jax.ShapeDtypeStruct((n, N_HEADS), _F32),
                   jax.ShapeDtypeStruct((n, n_rest), _F32)],
        compiler_params=_params(1),
        name="mla_sample_proj",
    )(x2, tab, norm_g.reshape(1, d), w_in, q_lora_g.reshape(1, q_lora), w_uq, q_gain,
      ckv_g.reshape(1, kv_lora), w_uk, w_ukt, k_gain)


def _mla_sample_attn_kernel(pt_ref, qabs_ref, qpe_ref, nckv_ref, nkpe_ref, nksc_ref, *rest, n_pages_step,
                            n_chains, dec):
    del pt_ref
    ckv_refs = rest[:n_pages_step]
    kpe_refs = rest[n_pages_step:2 * n_pages_step]
    ksc_refs = rest[2 * n_pages_step:3 * n_pages_step]
    o_ref, m_ref, l_ref, acc_ref = rest[3 * n_pages_step:]
    j = pl.program_id(1)
    rows = N_HEADS * dec
    qa = qabs_ref[0]
    qp = qpe_ref[0][:, :QK_ROPE]

    @pl.when(j == 0)
    def _():
        ckv = nckv_ref[0].astype(_BF16)
        rr = lax.broadcasted_iota(jnp.int32, (rows, N_HEADS), 0)
        hh = lax.broadcasted_iota(jnp.int32, (rows, N_HEADS), 1)
        expand = (rr % N_HEADS == hh).astype(_F32)
        scale = lax.dot_general(expand, nksc_ref[0], (((1,), (1,)), ((), ())), precision=lax.Precision.HIGHEST,
                                preferred_element_type=_F32)
        s = (_dot_nt(qa, ckv) + _dot_nt(qp, nkpe_ref[0].astype(_BF16))) * scale
        tq = lax.broadcasted_iota(jnp.int32, (rows, dec), 0) // N_HEADS
        tk = lax.broadcasted_iota(jnp.int32, (rows, dec), 1)
        s = jnp.where(tk <= tq, s, -jnp.inf)
        m = jnp.max(s, axis=-1, keepdims=True)
        p = jnp.exp2(s - m)
        m_ref[0] = m
        l_ref[0] = jnp.sum(p, axis=-1, keepdims=True)
        acc_ref[0] = _dot(p.astype(_BF16), ckv)
        for c in range(1, n_chains):
            m_ref[c] = jnp.full((rows, 1), -jnp.inf, _F32)
            l_ref[c] = jnp.zeros((rows, 1), _F32)
            acc_ref[c] = jnp.zeros(acc_ref.shape[1:], _F32)

    per_chain = n_pages_step // n_chains
    n_keys = per_chain * PAGE_SIZE
    for c in range(n_chains):
        pages = range(c * per_chain, (c + 1) * per_chain)
        ck = jnp.concatenate([ckv_refs[p][0].astype(_BF16) for p in pages], axis=0)
        kpt = jnp.concatenate([kpe_refs[p][0].astype(_BF16) for p in pages], axis=1)
        kst = jnp.concatenate([ksc_refs[p][0] for p in pages], axis=1)
        s = _dot_nt(qa, ck) + _dot(qp, kpt)
        s = (s.reshape(dec, N_HEADS, n_keys) * kst[None]).reshape(rows, n_keys)
        m_old = m_ref[c]
        m_new = jnp.maximum(m_old, jnp.max(s, axis=-1, keepdims=True))
        alpha = jnp.exp2(m_old - m_new)
        p = jnp.exp2(s - m_new)
        l_ref[c] = alpha * l_ref[c] + jnp.sum(p, axis=-1, keepdims=True)
        acc_ref[c] = alpha * acc_ref[c] + _dot(p.astype(_BF16), ck)
        m_ref[c] = m_new

    @pl.when(j == pl.num_programs(1) - 1)
    def _():
        m = m_ref[0]
        for c in range(1, n_chains):
            m = jnp.maximum(m, m_ref[c])
        l = jnp.zeros((rows, 1), _F32)
        acc = jnp.zeros(acc_ref.shape[1:], _F32)
        for c in range(n_chains):
            w = jnp.exp2(m_ref[c] - m)
            l = l + w * l_ref[c]
            acc = acc + w * acc_ref[c]
        o_ref[0] = acc / l


def _mla_sample_attn(page_table, qabs, qpe, nckv, nkpe, nksc, cache_ckv, cache_kpe_t, cache_ksc_t):
    b, rows, kv_lora = qabs.shape
    dec = nckv.shape[1]
    n_pages = page_table.shape[1]
    pps = math.gcd(n_pages, PAGES_PER_STEP)
    n_chains = math.gcd(pps, SAMPLE_CHAINS)
    kern = functools.partial(_mla_sample_attn_kernel, n_pages_step=pps, n_chains=n_chains, dec=dec)

    def page_spec(shape, p):
        return pl.BlockSpec((1,) + shape, lambda i, j, pt: (pt[i, j * pps + p], 0, 0))

    per_b = lambda shape: pl.BlockSpec((1,) + shape, lambda i, j, pt: (i, 0, 0))
    grid_spec = pltpu.PrefetchScalarGridSpec(
        num_scalar_prefetch=1,
        grid=(b, n_pages // pps),
        in_specs=([per_b((rows, kv_lora)), per_b((rows, LANES)), per_b((dec, kv_lora)), per_b((dec, QK_ROPE)),
                   per_b((dec, N_HEADS))]
                  + [page_spec((PAGE_SIZE, kv_lora), p) for p in range(pps)]
                  + [page_spec((QK_ROPE, PAGE_SIZE), p) for p in range(pps)]
                  + [page_spec((N_HEADS, PAGE_SIZE), p) for p in range(pps)]),
        out_specs=per_b((rows, kv_lora)),
        scratch_shapes=[pltpu.VMEM((n_chains, rows, 1), _F32), pltpu.VMEM((n_chains, rows, 1), _F32),
                        pltpu.VMEM((n_chains, rows, kv_lora), _F32)],
    )
    return pl.pallas_call(
        kern,
        grid_spec=grid_spec,
        out_shape=jax.ShapeDtypeStruct((b, rows, kv_lora), _F32),
        compiler_params=_params(2),
        name="mla_sample_attn",
    )(page_table, qabs, qpe, nckv, nkpe, nksc, *([cache_ckv] * pps), *([cache_kpe_t] * pps),
      *([cache_ksc_t] * pps))


def _mla_sample_out_kernel(x_ref, olat_ref, rest_ref, mk_ref, mv_ref, wuv_ref, wout_ref, qg_ref,
                           y_ref, obuf, *, nb, dec, kv_lora):
    d_mla = N_HEADS * V_HEAD
    zoff = D_MEM
    for h in range(N_HEADS):
        c0 = h * V_HEAD
        y_h = _dot(olat_ref[:, h * kv_lora:(h + 1) * kv_lora].astype(_BF16), wuv_ref[h])
        obuf[:, c0:c0 + V_HEAD] = (y_h * _silu(rest_ref[:, zoff + c0:zoff + c0 + V_HEAD])).astype(_BF16)
    qg = qg_ref[...]
    for bi in range(nb):
        r = slice(bi * dec, (bi + 1) * dec)
        outs = _mem_attend(rest_ref[r, :D_MEM], mk_ref, mv_ref, bi, qg)
        for h, om in enumerate(outs):
            c0 = d_mla + h * MEM_HEAD_DIM
            obuf[r, c0:c0 + MEM_HEAD_DIM] = (om * _silu(rest_ref[r, zoff + c0:zoff + c0 + MEM_HEAD_DIM])).astype(_BF16)
    y_ref[...] = x_ref[...] + _dot(obuf[...], wout_ref[...])


def _mla_sample_out(x2, olat, rest, mem_k, mem_v, mem_layer, mem_q_g, w_uv3, w_out, *, nb, dec):
    n, d = x2.shape
    kv_lora = w_uv3.shape[1]
    mem_rows = mem_k.shape[1]
    rows = nb * dec
    off = mem_layer * (n // rows)
    kern = functools.partial(_mla_sample_out_kernel, nb=nb, dec=dec, kv_lora=kv_lora)
    tok = lambda w: pl.BlockSpec((rows, w), lambda i: (i, 0))
    return pl.pallas_call(
        kern,
        grid=(n // rows,),
        in_specs=[
            tok(d), tok(olat.shape[1]), tok(rest.shape[1]),
            pl.BlockSpec((nb, mem_rows, MEM_HEAD_DIM), lambda i: (off + i, 0, 0)),
            pl.BlockSpec((nb, mem_rows, MEM_HEAD_DIM), lambda i: (off + i, 0, 0)),
            _const_spec(w_uv3.shape),
            _const_spec(w_out.shape),
            _const_spec((1, MEM_HEAD_DIM)),
        ],
        out_specs=tok(d),
        out_shape=jax.ShapeDtypeStruct((n, d), _F32),
        scratch_shapes=[pltpu.VMEM((rows, w_out.shape[0]), _BF16)],
        compiler_params=_params(1),
        name="mla_sample_out",
    )(x2, olat, rest, mem_k, mem_v, w_uv3, w_out, mem_q_g.reshape(1, MEM_HEAD_DIM))


def _pick_tile(n, target):
    t = min(n, target)
    while n % t or t % SUBLANES:
        t -= 1
    return t


def kernel(x_prompt, x_sample, mem_prompt, state_conv, cache_ckv, cache_kpe, cache_kscale, cache_mem_k, cache_mem_v, page_table, conv_norm_g, conv_w_in, conv_w, conv_w_out, mla_norm_g, mla_w_in, mla_q_lora_g, mla_w_uq, mla_ckv_g, mla_w_uk, mla_w_uv, mla_q_g, mla_k_g, mla_w_out, mem_norm_g, mem_w_kv, mem_q_g, mem_k_g):
    bp, seq, d = x_prompt.shape
    bs, dec, _ = x_sample.shape
    depth = mem_norm_g.shape[0]
    n_mem = mem_prompt.shape[1]
    dc = conv_w.shape[1]
    q_lora = mla_q_lora_g.shape[0]
    kv_lora = mla_ckv_g.shape[0]
    past = page_table.shape[1] * PAGE_SIZE

    tm_p = _pick_tile(seq, 512)
    nb_s = _pick_tile(bs, 8) if bs % 8 == 0 else bs

    mem_k, mem_v = _mem_kv(mem_prompt, mem_norm_g, mem_w_kv, mem_k_g)
    mem_rows = n_mem * MEM_HEADS
    mem_k2 = mem_k.reshape(depth * bp, mem_rows, MEM_HEAD_DIM)
    mem_v2 = mem_v.reshape(depth * bp, mem_rows, MEM_HEAD_DIM)
    smem_k2 = cache_mem_k.reshape(depth * bs, mem_rows, MEM_HEAD_DIM)
    smem_v2 = cache_mem_v.reshape(depth * bs, mem_rows, MEM_HEAD_DIM)

    cw_in = conv_w_in.astype(_BF16)
    cw_out = conv_w_out.astype(_BF16)
    hist_p = jnp.zeros((bp, CONV_WIDTH - 1, dc), _F32)
    y_p, conv_p = _conv_layer(x_prompt, hist_p, mem_k2, mem_v2, 0, mem_q_g[0], conv_norm_g, cw_in, conv_w, cw_out,
                              nb=1, tm=tm_p)
    y_s, conv_s = _conv_layer(x_sample, state_conv, smem_k2, smem_v2, 0, mem_q_g[0], conv_norm_g, cw_in, conv_w,
                              cw_out, nb=nb_s, tm=dec)

    w_in, w_uq, q_gain, k_gain = _mla_weights(mla_w_in, mla_w_uq, mla_q_g, mla_k_g, q_lora, kv_lora)
    w_uk2 = mla_w_uk.reshape(kv_lora, N_HEADS * QK_NOPE).astype(_BF16)
    w_uv2 = mla_w_uv.reshape(kv_lora, N_HEADS * V_HEAD).astype(_BF16)
    w_out = mla_w_out.astype(_BF16)
    tab_p = _rope_table(jnp.arange(seq, dtype=jnp.int32))
    y_p, ckv_p, kpe_p, ksc_p = _mla_prompt(y_p, tab_p, mem_k2, mem_v2, 1, mem_q_g[1], mla_norm_g, w_in,
                                           mla_q_lora_g, w_uq, q_gain, mla_ckv_g, w_uk2, w_uv2, k_gain, w_out,
                                           tm=tm_p)

    n_s = bs * dec
    tab_s = jnp.tile(_rope_table(past + jnp.arange(dec, dtype=jnp.int32)), (bs, 1))
    w_ukt = jnp.transpose(mla_w_uk, (1, 2, 0)).astype(_BF16)
    w_uv3 = jnp.transpose(mla_w_uv, (1, 0, 2)).astype(_BF16)
    qabs, qpe, ckv_s, kpe_s, ksc_s, rest = _mla_sample_proj(
        y_s.reshape(n_s, d), tab_s, mla_norm_g, w_in, mla_q_lora_g, w_uq, q_gain, mla_ckv_g, w_uk2, w_ukt, k_gain,
        tm=_pick_tile(n_s, 512))
    rows = dec * N_HEADS
    olat = _mla_sample_attn(page_table, qabs.reshape(bs, rows, kv_lora), qpe.reshape(bs, rows, LANES),
                            ckv_s.reshape(bs, dec, kv_lora), kpe_s.reshape(bs, dec, QK_ROPE),
                            ksc_s.reshape(bs, dec, N_HEADS), cache_ckv, jnp.transpose(cache_kpe, (0, 2, 1)),
                            jnp.transpose(cache_kscale, (0, 2, 1)))
    y_s2 = _mla_sample_out(y_s.reshape(n_s, d), olat.reshape(n_s, N_HEADS * kv_lora), rest, smem_k2, smem_v2, 1,
                           mem_q_g[1], w_uv3, w_out, nb=nb_s, dec=dec)

    new_shape = (depth, bp, n_mem, MEM_HEADS, MEM_HEAD_DIM)
    return (y_p, y_s2.reshape(bs, dec, d), conv_p, conv_s, ckv_p, kpe_p, ksc_p,
            ckv_s.reshape(bs, dec, kv_lora), kpe_s.reshape(bs, dec, QK_ROPE), ksc_s.reshape(bs, dec, N_HEADS),
            mem_k.reshape(new_shape), mem_v.reshape(new_shape))
```

```python
import functools
import math

import jax
import jax.numpy as jnp
from jax import lax
from jax.experimental import pallas as pl
from jax.experimental.pallas import tpu as pltpu

EPS = 1e-6
ROPE_THETA = 10000.0
PAGE_SIZE = 128
N_HEADS = 8
QK_NOPE = 128
QK_ROPE = 64
QK_HEAD = QK_NOPE + QK_ROPE
V_HEAD = 128
MEM_HEADS = 4
MEM_HEAD_DIM = 128
D_MEM = MEM_HEADS * MEM_HEAD_DIM
CONV_WIDTH = 3
SM_SCALE = QK_HEAD ** -0.5
MEM_SCALE = MEM_HEAD_DIM ** -0.5

LANES = 128
SUBLANES = 8
HEAD_PAD = 2 * LANES
VMEM_LIMIT = 60000 * 1024
PAGES_PER_STEP = 16
SAMPLE_BATCHES_PER_STEP = 16
ATTN_Q_ROWS = 256
HEADS_PER_TRIP = 4
LOG2E = math.log2(math.e)

_F32 = jnp.float32
_BF16 = jnp.bfloat16


def _dot(a, b):
    return jnp.dot(a, b, preferred_element_type=_F32)


def _dot_nt(a, b):
    return lax.dot_general(a, b, (((1,), (1,)), ((), ())), preferred_element_type=_F32)


def _rms(x, g):
    return x * lax.rsqrt(jnp.mean(x * x, axis=-1, keepdims=True) + EPS) * g


def _silu(z):
    return z / (1.0 + jnp.exp(-z))


def _const_spec(shape):
    nd = len(shape)
    return pl.BlockSpec(shape, lambda *_: (0,) * nd, pipeline_mode=pl.Buffered(1))


def _params(n_axes):
    return pltpu.CompilerParams(dimension_semantics=("arbitrary",) * n_axes,
                                vmem_limit_bytes=VMEM_LIMIT)


def _mem_attend(qm, k_ref, v_ref, bi, qg):
    rows = qm.shape[0]
    mem_rows = k_ref.shape[1]
    n_mem = mem_rows // MEM_HEADS
    heads = [slice(h * MEM_HEAD_DIM, (h + 1) * MEM_HEAD_DIM) for h in range(MEM_HEADS)]
    if rows * MEM_HEADS <= LANES:
        q = jnp.concatenate([_rms(qm[:, sl], qg) * (MEM_SCALE * LOG2E) for sl in heads], axis=0).astype(_BF16)
        s = _dot_nt(q, k_ref[bi].astype(_BF16))
        q_head = lax.broadcasted_iota(jnp.int32, s.shape, 0) // rows
        k_head = lax.broadcasted_iota(jnp.int32, s.shape, 1) % MEM_HEADS
        s = jnp.where(q_head == k_head, s, -jnp.inf)
        p = jnp.exp2(s - jnp.max(s, axis=-1, keepdims=True))
        o = _dot(p.astype(_BF16), v_ref[bi].astype(_BF16)) / jnp.sum(p, axis=-1, keepdims=True)
        return [o[h * rows:(h + 1) * rows] for h in range(MEM_HEADS)]
    outs = []
    for sl in heads:
        h = sl.start // MEM_HEAD_DIM
        q = (_rms(qm[:, sl], qg) * (MEM_SCALE * LOG2E)).astype(_BF16)
        k = k_ref[bi, pl.ds(h, n_mem, stride=MEM_HEADS), :].astype(_BF16)
        v = v_ref[bi, pl.ds(h, n_mem, stride=MEM_HEADS), :].astype(_BF16)
        s = _dot_nt(q, k)
        p = jnp.exp2(s - jnp.max(s, axis=-1, keepdims=True))
        o = _dot(p.astype(_BF16), v) / jnp.sum(p, axis=-1, keepdims=True)
        outs.append(o)
    return outs


def _mem_kv_kernel(mem_ref, ng_ref, w_ref, kg_ref, k_out, v_out):
    x = mem_ref[0]
    xn = _rms(x, ng_ref[0]).astype(_BF16)
    kv = _dot(xn, w_ref[0])
    kg = kg_ref[0]
    m = x.shape[0]
    for h in range(MEM_HEADS):
        sl = slice(h * MEM_HEAD_DIM, (h + 1) * MEM_HEAD_DIM)
        rows = pl.ds(h, m, stride=MEM_HEADS)
        k_out[0, 0, rows, :] = _rms(kv[:, sl], kg)
        v_out[0, 0, rows, :] = kv[:, D_MEM + h * MEM_HEAD_DIM:D_MEM + (h + 1) * MEM_HEAD_DIM]


def _mem_kv(mem, norm_g, w_kv, k_g):
    b, m, d = mem.shape
    depth = norm_g.shape[0]
    out = jax.ShapeDtypeStruct((depth, b, m * MEM_HEADS, MEM_HEAD_DIM), _F32)
    return pl.pallas_call(
        _mem_kv_kernel,
        grid=(depth, b),
        in_specs=[
            pl.BlockSpec((1, m, d), lambda l, i: (i, 0, 0)),
            pl.BlockSpec((1, 1, d), lambda l, i: (l, 0, 0)),
            pl.BlockSpec((1, d, 2 * D_MEM), lambda l, i: (l, 0, 0)),
            pl.BlockSpec((1, 1, MEM_HEAD_DIM), lambda l, i: (l, 0, 0)),
        ],
        out_specs=[pl.BlockSpec((1, 1, m * MEM_HEADS, MEM_HEAD_DIM), lambda l, i: (l, i, 0, 0))] * 2,
        out_shape=[out, out],
        compiler_params=_params(2),
        name="mem_kv",
    )(mem, norm_g.reshape(depth, 1, d), w_kv.astype(_BF16), k_g.reshape(depth, 1, MEM_HEAD_DIM))


def _conv_layer_kernel(x_ref, hist_ref, mk_ref, mv_ref, ng_ref, win_ref, cw_ref, wout_ref, qg_ref,
                       y_ref, state_ref, cbuf, obuf, *, nb, tm, dc):
    rows = nb * tm
    d = x_ref.shape[-1]

    @pl.when(pl.program_id(1) == 0)
    def _():
        cbuf[:, SUBLANES - 2:SUBLANES, :] = hist_ref[...]

    x = x_ref[...].reshape(rows, d)
    xn = _rms(x, ng_ref[...]).astype(_BF16)
    proj = _dot(xn, win_ref[...])
    cu = proj[:, 2 * dc:3 * dc] * proj[:, :dc]
    cbuf[:, SUBLANES:SUBLANES + tm, :] = cu.reshape(nb, tm, dc)
    cw = cw_ref[...]
    conv = (cw[0:1] * cbuf[:, SUBLANES - 2:SUBLANES - 2 + tm, :].reshape(rows, dc)
            + cw[1:2] * cbuf[:, SUBLANES - 1:SUBLANES - 1 + tm, :].reshape(rows, dc)
            + cw[2:3] * cu)
    state = cbuf[:, tm + SUBLANES - 2:tm + SUBLANES, :]
    state_ref[...] = state
    cbuf[:, SUBLANES - 2:SUBLANES, :] = state
    zoff = 3 * dc + D_MEM
    obuf[:, :dc] = (proj[:, dc:2 * dc] * conv * _silu(proj[:, zoff:zoff + dc])).astype(_BF16)
    qg = qg_ref[...]
    for bi in range(nb):
        r = slice(bi * tm, (bi + 1) * tm)
        outs = _mem_attend(proj[r, 3 * dc:3 * dc + D_MEM], mk_ref, mv_ref, bi, qg)
        for h, o in enumerate(outs):
            c0 = dc + h * MEM_HEAD_DIM
            obuf[r, c0:c0 + MEM_HEAD_DIM] = (o * _silu(proj[r, zoff + c0:zoff + c0 + MEM_HEAD_DIM])).astype(_BF16)
    y_ref[...] = (x + _dot(obuf[...], wout_ref[...])).reshape(nb, tm, d)


def _conv_layer(x, hist, mem_k, mem_v, mem_layer, mem_q_g, norm_g, w_in, conv_w, w_out, *, nb, tm):
    b, t, d = x.shape
    dc = conv_w.shape[1]
    mem_rows = mem_k.shape[1]
    off = mem_layer * (b // nb)
    kern = functools.partial(_conv_layer_kernel, nb=nb, tm=tm, dc=dc)
    return pl.pallas_call(
        kern,
        grid=(b // nb, t // tm),
        in_specs=[
            pl.BlockSpec((nb, tm, d), lambda i, j: (i, j, 0)),
            pl.BlockSpec((nb, CONV_WIDTH - 1, dc), lambda i, j: (i, 0, 0)),
            pl.BlockSpec((nb, mem_rows, MEM_HEAD_DIM), lambda i, j: (off + i, 0, 0)),
            pl.BlockSpec((nb, mem_rows, MEM_HEAD_DIM), lambda i, j: (off + i, 0, 0)),
            _const_spec((1, d)),
            _const_spec(w_in.shape),
            _const_spec(conv_w.shape),
            _const_spec(w_out.shape),
            _const_spec((1, MEM_HEAD_DIM)),
        ],
        out_specs=[
            pl.BlockSpec((nb, tm, d), lambda i, j: (i, j, 0)),
            pl.BlockSpec((nb, CONV_WIDTH - 1, dc), lambda i, j: (i, 0, 0)),
        ],
        out_shape=[jax.ShapeDtypeStruct((b, t, d), _F32),
                   jax.ShapeDtypeStruct((b, CONV_WIDTH - 1, dc), _F32)],
        scratch_shapes=[pltpu.VMEM((nb, tm + SUBLANES, dc), _F32),
                        pltpu.VMEM((nb * tm, dc + D_MEM), _BF16)],
        compiler_params=_params(2),
        name="conv_layer",
    )(x, hist, mem_k, mem_v, norm_g.reshape(1, d), w_in, conv_w, w_out, mem_q_g.reshape(1, MEM_HEAD_DIM))


def _rope_table(pos):
    inv = 1.0 / (ROPE_THETA ** (jnp.arange(0, QK_ROPE, 2, dtype=_F32) / QK_ROPE))
    ang = pos.astype(_F32)[:, None] * inv[None, :]
    c, s = jnp.cos(ang), jnp.sin(ang)
    return jnp.concatenate([c, c, -s, s], axis=-1)


def _swap_halves(w):
    half = w.shape[-1] // 2
    return jnp.concatenate([w[..., half:], w[..., :half]], axis=-1)


def _mla_weights(mla_w_in, mla_w_uq, mla_q_g, mla_k_g, q_lora, kv_lora):
    o = q_lora + kv_lora
    kpe_w = mla_w_in[:, o:o + QK_ROPE]
    w_in = jnp.concatenate([mla_w_in[:, :o], kpe_w, _swap_halves(kpe_w), mla_w_in[:, o + QK_ROPE:]], axis=1)
    wq = mla_w_uq.reshape(q_lora, N_HEADS, QK_HEAD)
    wq = jnp.concatenate([wq, _swap_halves(wq[..., QK_NOPE:])], axis=-1)
    w_uq = wq.reshape(q_lora, N_HEADS * HEAD_PAD)
    g_rope = mla_q_g[QK_NOPE:]
    q_gain = jnp.concatenate([mla_q_g[:QK_NOPE] * mla_k_g[:QK_NOPE], g_rope, _swap_halves(g_rope)])
    k_rope = mla_k_g[QK_NOPE:]
    k_gain = jnp.concatenate([k_rope, _swap_halves(k_rope)])
    return (w_in.astype(_BF16), w_uq.astype(_BF16), q_gain.reshape(1, HEAD_PAD), k_gain.reshape(1, LANES))


def _rope_lane_mask(rows):
    return lax.broadcasted_iota(jnp.int32, (rows, LANES), 1) < QK_ROPE


def _q_heads(cq, qlg, wuq_ref, q_gain, tab, rope_mask):
    cqn = _rms(cq, qlg).astype(_BF16)
    for h in range(N_HEADS):
        q = _dot(cqn, wuq_ref[:, h * HEAD_PAD:(h + 1) * HEAD_PAD])
        a = q[:, :LANES]
        b = q[:, LANES:]
        ss = (jnp.sum(a * a, axis=-1, keepdims=True)
              + jnp.sum(jnp.where(rope_mask, b * b, 0.0), axis=-1, keepdims=True))
        r = lax.rsqrt(ss / QK_HEAD + EPS)
        yield a * r * q_gain[:, :LANES], b * r * q_gain[:, LANES:] * tab


def _kpe_rot(kpe_blk, k_gain, tab):
    kt = kpe_blk * k_gain * tab
    return kt + pltpu.roll(kt, QK_ROPE, axis=1)


def _head_select(cols, rows):
    lane = lax.broadcasted_iota(jnp.int32, (rows, N_HEADS), 1)
    out = jnp.zeros((rows, N_HEADS), _F32)
    for h, c in enumerate(cols):
        out = jnp.where(lane == h, c, out)
    return out


def _mla_prompt_kernel(x_ref, tab_ref, mk_ref, mv_ref, ng_ref, win_ref, qlg_ref, wuq_ref, qgain_ref,
                       ckvg_ref, wuk_ref, wuv_ref, kgain_ref, wout_ref, qg_ref,
                       y_ref, ckv_ref, kpe_ref, ksc_ref,
                       late, qbuf, kbuf, vbuf, obuf, o3, *, tm, tq, q_lora, kv_lora):
    i = pl.program_id(1)
    x = x_ref[0]
    tab = tab_ref[...]
    rope_mask = _rope_lane_mask(tm)
    xn = _rms(x, ng_ref[...]).astype(_BF16)
    o = q_lora + kv_lora
    early = _dot(xn, win_ref[:, :o + LANES])
    late[...] = _dot(xn, win_ref[:, o + LANES:])

    heads = _q_heads(early[:, :q_lora], qlg_ref[...], wuq_ref, qgain_ref[...] * (SM_SCALE * LOG2E), tab, rope_mask)
    for h, (a, b) in enumerate(heads):
        qbuf[h, :, :LANES] = a.astype(_BF16)
        qbuf[h, :, LANES:] = b.astype(_BF16)

    ckv = _rms(early[:, q_lora:o], ckvg_ref[...])
    ckv_ref[0] = ckv
    ckv_b = ckv.astype(_BF16)
    kpe_blk = early[:, o:o + LANES]
    skpe = jnp.sum(jnp.where(rope_mask, kpe_blk * kpe_blk, 0.0), axis=-1, keepdims=True)
    rot = _kpe_rot(kpe_blk, kgain_ref[...], tab)
    kpe_ref[0] = rot[:, :QK_ROPE]
    row0 = pl.multiple_of(i * tm, tm)
    kscs = []
    for h2 in range(0, N_HEADS, 2):
        k_pair = _dot(ckv_b, wuk_ref[:, h2 * QK_NOPE:(h2 + 2) * QK_NOPE])
        v_pair = _dot(ckv_b, wuv_ref[:, h2 * V_HEAD:(h2 + 2) * V_HEAD])
        for h in (h2, h2 + 1):
            kn = k_pair[:, (h - h2) * QK_NOPE:(h - h2 + 1) * QK_NOPE]
            ksc = lax.rsqrt((jnp.sum(kn * kn, axis=-1, keepdims=True) + skpe) / QK_HEAD + EPS)
            kscs.append(ksc)
            kbuf[h, pl.ds(row0, tm), :LANES] = (kn * ksc).astype(_BF16)
            kbuf[h, pl.ds(row0, tm), LANES:] = (rot * ksc).astype(_BF16)
            vbuf[h, pl.ds(row0, tm), :V_HEAD] = v_pair[:, (h - h2) * V_HEAD:(h - h2 + 1) * V_HEAD].astype(_BF16)
            vbuf[h, pl.ds(row0, tm), V_HEAD:] = jnp.ones((tm, LANES), _BF16)
    ksc_ref[0] = _head_select(kscs, tm)

    causal = (lax.broadcasted_iota(jnp.int32, (tq, tq), 0) >= lax.broadcasted_iota(jnp.int32, (tq, tq), 1))

    def attend_block(h, r0, n_keys):
        q = qbuf[h, r0:r0 + tq, :]
        s = _dot_nt(q, kbuf[h, 0:n_keys, :])
        diag = jnp.where(causal, s[:, n_keys - tq:], -jnp.inf)
        m = jnp.max(diag, axis=-1, keepdims=True)
        if n_keys > tq:
            past = s[:, :n_keys - tq]
            m = jnp.maximum(m, jnp.max(past, axis=-1, keepdims=True))
            p = jnp.concatenate([jnp.exp2(past - m), jnp.exp2(diag - m)], axis=1)
        else:
            p = jnp.exp2(diag - m)
        ov = _dot(p.astype(_BF16), vbuf[h, 0:n_keys, :])
        o3[h, r0:r0 + tq, :] = (ov[:, :V_HEAD] / ov[:, V_HEAD:]).astype(o3.dtype)

    for ti in range(kbuf.shape[1] // tm):
        @pl.when(i == ti)
        def _(ti=ti):
            def head_body(hp, carry):
                for r0 in range(0, tm, tq):
                    for dh in range(HEADS_PER_TRIP):
                        attend_block(HEADS_PER_TRIP * hp + dh, r0, ti * tm + r0 + tq)
                return carry

            lax.fori_loop(0, N_HEADS // HEADS_PER_TRIP, head_body, 0)

    d_mla = N_HEADS * V_HEAD
    zoff = D_MEM
    for h in range(N_HEADS):
        c0 = h * V_HEAD
        obuf[:, c0:c0 + V_HEAD] = (o3[h].astype(_F32) * _silu(late[:, zoff + c0:zoff + c0 + V_HEAD])).astype(_BF16)
    outs = _mem_attend(late[:, :D_MEM], mk_ref, mv_ref, 0, qg_ref[...])
    for h, om in enumerate(outs):
        c0 = d_mla + h * MEM_HEAD_DIM
        obuf[:, c0:c0 + MEM_HEAD_DIM] = (om * _silu(late[:, zoff + c0:zoff + c0 + MEM_HEAD_DIM])).astype(_BF16)
    y_ref[0] = x + _dot(obuf[...], wout_ref[...])


def _mla_prompt(x, tab, mem_k, mem_v, mem_layer, mem_q_g, norm_g, w_in, q_lora_g, w_uq, q_gain, ckv_g,
                w_uk, w_uv, k_gain, w_out, *, tm):
    b, t, d = x.shape
    q_lora = q_lora_g.shape[0]
    kv_lora = ckv_g.shape[0]
    mem_rows = mem_k.shape[1]
    off = mem_layer * b
    d_out = N_HEADS * V_HEAD + D_MEM
    tq = math.gcd(tm, ATTN_Q_ROWS)
    kern = functools.partial(_mla_prompt_kernel, tm=tm, tq=tq, q_lora=q_lora, kv_lora=kv_lora)
    tok = lambda w: pl.BlockSpec((1, tm, w), lambda i, j: (i, j, 0))
    return pl.pallas_call(
        kern,
        grid=(b, t // tm),
        in_specs=[
            tok(d),
            pl.BlockSpec((tm, LANES), lambda i, j: (j, 0)),
            pl.BlockSpec((1, mem_rows, MEM_HEAD_DIM), lambda i, j: (off + i, 0, 0), pipeline_mode=pl.Buffered(1)),
            pl.BlockSpec((1, mem_rows, MEM_HEAD_DIM), lambda i, j: (off + i, 0, 0), pipeline_mode=pl.Buffered(1)),
            _const_spec((1, d)),
            _const_spec(w_in.shape),
            _const_spec((1, q_lora)),
            _const_spec(w_uq.shape),
            _const_spec((1, HEAD_PAD)),
            _const_spec((1, kv_lora)),
            _const_spec(w_uk.shape),
            _const_spec(w_uv.shape),
            _const_spec((1, LANES)),
            _const_spec(w_out.shape),
            _const_spec((1, MEM_HEAD_DIM)),
        ],
        out_specs=[tok(d), tok(kv_lora), tok(QK_ROPE), tok(N_HEADS)],
        out_shape=[jax.ShapeDtypeStruct((b, t, d), _F32),
                   jax.ShapeDtypeStruct((b, t, kv_lora), _F32),
                   jax.ShapeDtypeStruct((b, t, QK_ROPE), _F32),
                   jax.ShapeDtypeStruct((b, t, N_HEADS), _F32)],
        scratch_shapes=[
            pltpu.VMEM((tm, D_MEM + d_out), _F32),
            pltpu.VMEM((N_HEADS, tm, HEAD_PAD), _BF16),
            pltpu.VMEM((N_HEADS, t, HEAD_PAD), _BF16),
            pltpu.VMEM((N_HEADS, t, V_HEAD + LANES), _BF16),
            pltpu.VMEM((tm, d_out), _BF16),
            pltpu.VMEM((N_HEADS, tm, V_HEAD), _BF16),
        ],
        compiler_params=_params(2),
        name="mla_prompt",
    )(x, tab, mem_k, mem_v, norm_g.reshape(1, d), w_in, q_lora_g.reshape(1, q_lora), w_uq, q_gain,
      ckv_g.reshape(1, kv_lora), w_uk, w_uv, k_gain, w_out, mem_q_g.reshape(1, MEM_HEAD_DIM))


def _mla_sample_proj_kernel(x_ref, tab_ref, ng_ref, win_ref, qlg_ref, wuq_ref, qgain_ref, ckvg_ref,
                            wuk_ref, wukt_ref, kgain_ref,
                            qabs_ref, qpe_ref, ckv_ref, kpe_ref, ksc_ref, rest_ref, *, q_lora, kv_lora):
    rows = x_ref.shape[0]
    tab = tab_ref[...]
    rope_mask = _rope_lane_mask(rows)
    xn = _rms(x_ref[...], ng_ref[...]).astype(_BF16)
    proj = _dot(xn, win_ref[...])
    o = q_lora + kv_lora
    rest_ref[...] = proj[:, o + LANES:]

    heads = _q_heads(proj[:, :q_lora], qlg_ref[...], wuq_ref, qgain_ref[...] * (SM_SCALE * LOG2E), tab, rope_mask)
    for h, (a, b) in enumerate(heads):
        qabs_ref[:, h * kv_lora:(h + 1) * kv_lora] = _dot(a.astype(_BF16), wukt_ref[h]).astype(_BF16)
        qpe_ref[:, h * LANES:(h + 1) * LANES] = (b + pltpu.roll(b, QK_ROPE, axis=1)).astype(_BF16)

    ckv = _rms(proj[:, q_lora:o], ckvg_ref[...])
    ckv_ref[...] = ckv
    k_nope = _dot(ckv.astype(_BF16), wuk_ref[...])
    kpe_blk = proj[:, o:o + LANES]
    skpe = jnp.sum(jnp.where(rope_mask, kpe_blk * kpe_blk, 0.0), axis=-1, keepdims=True)
    kpe_ref[...] = _kpe_rot(kpe_blk, kgain_ref[...], tab)[:, :QK_ROPE]
    kscs = []
    for h in range(N_HEADS):
        kn = k_nope[:, h * QK_NOPE:(h + 1) * QK_NOPE]
        kscs.append(lax.rsqrt((jnp.sum(kn * kn, axis=-1, keepdims=True) + skpe) / QK_HEAD + EPS))
    ksc_ref[...] = _head_select(kscs, rows)


def _mla_sample_proj(x2, tab, norm_g, w_in, q_lora_g, w_uq, q_gain, ckv_g, w_uk, w_ukt, k_gain, *, tm):
    n, d = x2.shape
    q_lora = q_lora_g.shape[0]
    kv_lora = ckv_g.shape[0]
    n_rest = w_in.shape[1] - (q_lora + kv_lora + LANES)
    kern = functools.partial(_mla_sample_proj_kernel, q_lora=q_lora, kv_lora=kv_lora)
    tok = lambda w: pl.BlockSpec((tm, w), lambda i: (i, 0))
    return pl.pallas_call(
        kern,
        grid=(n // tm,),
        in_specs=[
            tok(d), tok(LANES),
            _const_spec((1, d)),
            _const_spec(w_in.shape),
            _const_spec((1, q_lora)),
            _const_spec(w_uq.shape),
            _const_spec((1, HEAD_PAD)),
            _const_spec((1, kv_lora)),
            _const_spec(w_uk.shape),
            _const_spec(w_ukt.shape),
            _const_spec((1, LANES)),
        ],
        out_specs=[tok(N_HEADS * kv_lora), tok(N_HEADS * LANES), tok(kv_lora), tok(QK_ROPE), tok(N_HEADS),
                   tok(n_rest)],
        out_shape=[jax.ShapeDtypeStruct((n, N_HEADS * kv_lora), _BF16),
                   jax.ShapeDtypeStruct((n, N_HEADS * LANES), _BF16),
                   jax.ShapeDtypeStruct((n, kv_lora), _F32),
                   jax.ShapeDtypeStruct((n, QK_ROPE), _F32),
                   jax.ShapeDtypeStruct((n, N_HEADS), _F32),
                   jax.ShapeDtypeStruct((n, n_rest), _F32)],
        compiler_params=_params(1),
        name="mla_sample_proj",
    )(x2, tab, norm_g.reshape(1, d), w_in, q_lora_g.reshape(1, q_lora), w_uq, q_gain,
      ckv_g.reshape(1, kv_lora), w_uk, w_ukt, k_gain)


def _mla_sample_attn_kernel(pt_ref, qabs_ref, qpe_ref, nckv_ref, nkpe_ref, nksc_ref, ckv_hbm, kpe_hbm, ksc_hbm,
                            o_ref, ckv_buf, kpe_buf, ksc_buf, sems, ck_all, s_all, *, pps, n_blocks, dec):
    b = pl.program_id(0)
    rows = N_HEADS * dec
    blk_keys = pps * PAGE_SIZE

    def page_copies(bb, blk, slot):
        copies = []
        for p in range(pps):
            page = pt_ref[bb, blk * pps + p]
            keys = pl.ds(p * PAGE_SIZE, PAGE_SIZE)
            copies.append(pltpu.make_async_copy(ckv_hbm.at[page], ckv_buf.at[slot, keys, :], sems.at[0, slot]))
            copies.append(pltpu.make_async_copy(kpe_hbm.at[page], kpe_buf.at[slot, :, keys], sems.at[1, slot]))
            copies.append(pltpu.make_async_copy(ksc_hbm.at[page], ksc_buf.at[slot, :, keys], sems.at[2, slot]))
        return copies

    def slot_of(blk):
        return blk % 2 if n_blocks % 2 == 0 else lax.rem(b * n_blocks + blk, 2)

    @pl.when(b == 0)
    def _():
        for c in page_copies(0, 0, 0):
            c.start()

    qa = qabs_ref[0]
    qp = qpe_ref[0][:, :QK_ROPE]
    mx = jnp.full((rows, LANES), -jnp.inf, _F32)
    for blk in range(n_blocks):
        slot = slot_of(blk)
        if blk + 1 < n_blocks:
            for c in page_copies(b, blk + 1, slot_of(blk + 1)):
                c.start()
        else:
            @pl.when(b + 1 < pl.num_programs(0))
            def _():
                for c in page_copies(b + 1, 0, slot_of(n_blocks)):
                    c.start()
        for c in page_copies(b, blk, slot):
            c.wait()
        ck = ckv_buf[slot].astype(_BF16)
        ck_all[blk * blk_keys:(blk + 1) * blk_keys, :] = ck
        s = _dot_nt(qa, ck) + _dot(qp, kpe_buf[slot].astype(_BF16))
        s = (s.reshape(dec, N_HEADS, blk_keys) * ksc_buf[slot][None]).reshape(rows, blk_keys)
        s_all[:, blk * blk_keys:(blk + 1) * blk_keys] = s
        for t in range(blk_keys // LANES):
            mx = jnp.maximum(mx, s[:, t * LANES:(t + 1) * LANES])

    nckv = nckv_ref[0].astype(_BF16)
    rr = lax.broadcasted_iota(jnp.int32, (rows, N_HEADS), 0)
    hh = lax.broadcasted_iota(jnp.int32, (rows, N_HEADS), 1)
    expand = (rr % N_HEADS == hh).astype(_F32)
    scale = lax.dot_general(expand, nksc_ref[0], (((1,), (1,)), ((), ())), precision=lax.Precision.HIGHEST,
                            preferred_element_type=_F32)
    s_new = (_dot_nt(qa, nckv) + _dot_nt(qp, nkpe_ref[0].astype(_BF16))) * scale
    tq = lax.broadcasted_iota(jnp.int32, (rows, dec), 0) // N_HEADS
    tk = lax.broadcasted_iota(jnp.int32, (rows, dec), 1)
    s_new = jnp.where(tk <= tq, s_new, -jnp.inf)

    m = jnp.maximum(jnp.max(mx, axis=-1, keepdims=True), jnp.max(s_new, axis=-1, keepdims=True))
    p_new = jnp.exp2(s_new - m)
    p = jnp.exp2(s_all[...] - m)
    l = jnp.sum(p, axis=-1, keepdims=True) + jnp.sum(p_new, axis=-1, keepdims=True)
    acc = _dot(p.astype(_BF16), ck_all[...]) + _dot(p_new.astype(_BF16), nckv)
    o_ref[0] = acc / l


def _mla_sample_attn(page_table, qabs, qpe, nckv, nkpe, nksc, cache_ckv, cache_kpe_t, cache_ksc_t):
    b, rows, kv_lora = qabs.shape
    dec = nckv.shape[1]
    n_pages = page_table.shape[1]
    pps = math.gcd(n_pages, PAGES_PER_STEP)
    n_blocks = n_pages // pps
    blk_keys = pps * PAGE_SIZE
    n_keys = n_pages * PAGE_SIZE
    kern = functools.partial(_mla_sample_attn_kernel, pps=pps, n_blocks=n_blocks, dec=dec)
    per_b = lambda shape: pl.BlockSpec((1,) + shape, lambda i, pt: (i, 0, 0))
    hbm = pl.BlockSpec(memory_space=pl.ANY)
    grid_spec = pltpu.PrefetchScalarGridSpec(
        num_scalar_prefetch=1,
        grid=(b,),
        in_specs=[per_b((rows, kv_lora)), per_b((rows, LANES)), per_b((dec, kv_lora)), per_b((dec, QK_ROPE)),
                  per_b((dec, N_HEADS)), hbm, hbm, hbm],
        out_specs=per_b((rows, kv_lora)),
        scratch_shapes=[
            pltpu.VMEM((2, blk_keys, kv_lora), _F32),
            pltpu.VMEM((2, QK_ROPE, blk_keys), _F32),
            pltpu.VMEM((2, N_HEADS, blk_keys), _F32),
            pltpu.SemaphoreType.DMA((3, 2)),
            pltpu.VMEM((n_keys, kv_lora), _BF16),
            pltpu.VMEM((rows, n_keys), _F32),
        ],
    )
    return pl.pallas_call(
        kern,
        grid_spec=grid_spec,
        out_shape=jax.ShapeDtypeStruct((b, rows, kv_lora), _F32),
        compiler_params=_params(1),
        name="mla_sample_attn",
    )(page_table, qabs, qpe, nckv, nkpe, nksc, cache_ckv, cache_kpe_t, cache_ksc_t)


def _mla_sample_out_kernel(x_ref, olat_ref, rest_ref, mk_ref, mv_ref, wuv_ref, wout_ref, qg_ref,
                           y_ref, obuf, *, nb, dec, kv_lora):
    d_mla = N_HEADS * V_HEAD
    zoff = D_MEM
    for h in range(N_HEADS):
        c0 = h * V_HEAD
        y_h = _dot(olat_ref[:, h * kv_lora:(h + 1) * kv_lora].astype(_BF16), wuv_ref[h])
        obuf[:, c0:c0 + V_HEAD] = (y_h * _silu(rest_ref[:, zoff + c0:zoff + c0 + V_HEAD])).astype(_BF16)
    qg = qg_ref[...]
    for bi in range(nb):
        r = slice(bi * dec, (bi + 1) * dec)
        outs = _mem_attend(rest_ref[r, :D_MEM], mk_ref, mv_ref, bi, qg)
        for h, om in enumerate(outs):
            c0 = d_mla + h * MEM_HEAD_DIM
            obuf[r, c0:c0 + MEM_HEAD_DIM] = (om * _silu(rest_ref[r, zoff + c0:zoff + c0 + MEM_HEAD_DIM])).astype(_BF16)
    y_ref[...] = x_ref[...] + _dot(obuf[...], wout_ref[...])


def _mla_sample_out(x2, olat, rest, mem_k, mem_v, mem_layer, mem_q_g, w_uv3, w_out, *, nb, dec):
    n, d = x2.shape
    kv_lora = w_uv3.shape[1]
    mem_rows = mem_k.shape[1]
    rows = nb * dec
    off = mem_layer * (n // rows)
    kern = functools.partial(_mla_sample_out_kernel, nb=nb, dec=dec, kv_lora=kv_lora)
    tok = lambda w: pl.BlockSpec((rows, w), lambda i: (i, 0))
    return pl.pallas_call(
        kern,
        grid=(n // rows,),
        in_specs=[
            tok(d), tok(olat.shape[1]), tok(rest.shape[1]),
            pl.BlockSpec((nb, mem_rows, MEM_HEAD_DIM), lambda i: (off + i, 0, 0)),
            pl.BlockSpec((nb, mem_rows, MEM_HEAD_DIM), lambda i: (off + i, 0, 0)),
            _const_spec(w_uv3.shape),
            _const_spec(w_out.shape),
            _const_spec((1, MEM_HEAD_DIM)),
        ],
        out_specs=tok(d),
        out_shape=jax.ShapeDtypeStruct((n, d), _F32),
        scratch_shapes=[pltpu.VMEM((rows, w_out.shape[0]), _BF16)],
        compiler_params=_params(1),
        name="mla_sample_out",
    )(x2, olat, rest, mem_k, mem_v, w_uv3, w_out, mem_q_g.reshape(1, MEM_HEAD_DIM))


def _pick_tile(n, target):
    t = min(n, target)
    while n % t or t % SUBLANES:
        t -= 1
    return t


def kernel(x_prompt, x_sample, mem_prompt, state_conv, cache_ckv, cache_kpe, cache_kscale, cache_mem_k, cache_mem_v, page_table, conv_norm_g, conv_w_in, conv_w, conv_w_out, mla_norm_g, mla_w_in, mla_q_lora_g, mla_w_uq, mla_ckv_g, mla_w_uk, mla_w_uv, mla_q_g, mla_k_g, mla_w_out, mem_norm_g, mem_w_kv, mem_q_g, mem_k_g):
    bp, seq, d = x_prompt.shape
    bs, dec, _ = x_sample.shape
    depth = mem_norm_g.shape[0]
    n_mem = mem_prompt.shape[1]
    dc = conv_w.shape[1]
    q_lora = mla_q_lora_g.shape[0]
    kv_lora = mla_ckv_g.shape[0]
    past = page_table.shape[1] * PAGE_SIZE

    tm_p = _pick_tile(seq, 512)
    nb_s = _pick_tile(bs, SAMPLE_BATCHES_PER_STEP) if bs % SUBLANES == 0 else bs

    mem_k, mem_v = _mem_kv(mem_prompt, mem_norm_g, mem_w_kv, mem_k_g)
    mem_rows = n_mem * MEM_HEADS
    mem_k2 = mem_k.reshape(depth * bp, mem_rows, MEM_HEAD_DIM)
    mem_v2 = mem_v.reshape(depth * bp, mem_rows, MEM_HEAD_DIM)
    smem_k2 = cache_mem_k.reshape(depth * bs, mem_rows, MEM_HEAD_DIM)
    smem_v2 = cache_mem_v.reshape(depth * bs, mem_rows, MEM_HEAD_DIM)

    cw_in = conv_w_in.astype(_BF16)
    cw_out = conv_w_out.astype(_BF16)
    hist_p = jnp.zeros((bp, CONV_WIDTH - 1, dc), _F32)
    y_p, conv_p = _conv_layer(x_prompt, hist_p, mem_k2, mem_v2, 0, mem_q_g[0], conv_norm_g, cw_in, conv_w, cw_out,
                              nb=1, tm=tm_p)
    y_s, conv_s = _conv_layer(x_sample, state_conv, smem_k2, smem_v2, 0, mem_q_g[0], conv_norm_g, cw_in, conv_w,
                              cw_out, nb=nb_s, tm=dec)

    w_in, w_uq, q_gain, k_gain = _mla_weights(mla_w_in, mla_w_uq, mla_q_g, mla_k_g, q_lora, kv_lora)
    w_uk2 = mla_w_uk.reshape(kv_lora, N_HEADS * QK_NOPE).astype(_BF16)
    w_uv2 = mla_w_uv.reshape(kv_lora, N_HEADS * V_HEAD).astype(_BF16)
    w_out = mla_w_out.astype(_BF16)
    tab_p = _rope_table(jnp.arange(seq, dtype=jnp.int32))
    y_p, ckv_p, kpe_p, ksc_p = _mla_prompt(y_p, tab_p, mem_k2, mem_v2, 1, mem_q_g[1], mla_norm_g, w_in,
                                           mla_q_lora_g, w_uq, q_gain, mla_ckv_g, w_uk2, w_uv2, k_gain, w_out,
                                           tm=tm_p)

    n_s = bs * dec
    tab_s = jnp.tile(_rope_table(past + jnp.arange(dec, dtype=jnp.int32)), (bs, 1))
    w_ukt = jnp.transpose(mla_w_uk, (1, 2, 0)).astype(_BF16)
    w_uv3 = jnp.transpose(mla_w_uv, (1, 0, 2)).astype(_BF16)
    qabs, qpe, ckv_s, kpe_s, ksc_s, rest = _mla_sample_proj(
        y_s.reshape(n_s, d), tab_s, mla_norm_g, w_in, mla_q_lora_g, w_uq, q_gain, mla_ckv_g, w_uk2, w_ukt, k_gain,
        tm=_pick_tile(n_s, 512))
    rows = dec * N_HEADS
    olat = _mla_sample_attn(page_table, qabs.reshape(bs, rows, kv_lora), qpe.reshape(bs, rows, LANES),
                            ckv_s.reshape(bs, dec, kv_lora), kpe_s.reshape(bs, dec, QK_ROPE),
                            ksc_s.reshape(bs, dec, N_HEADS), cache_ckv, jnp.transpose(cache_kpe, (0, 2, 1)),
                            jnp.transpose(cache_kscale, (0, 2, 1)))
    y_s2 = _mla_sample_out(y_s.reshape(n_s, d), olat.reshape(n_s, N_HEADS * kv_lora), rest, smem_k2, smem_v2, 1,
                           mem_q_g[1], w_uv3, w_out, nb=nb_s, dec=dec)

    new_shape = (depth, bp, n_mem, MEM_HEADS, MEM_HEAD_DIM)
    return (y_p, y_s2.reshape(bs, dec, d), conv_p, conv_s, ckv_p, kpe_p, ksc_p,
            ckv_s.reshape(bs, dec, kv_lora), kpe_s.reshape(bs, dec, QK_ROPE), ksc_s.reshape(bs, dec, N_HEADS),
            mem_k.reshape(new_shape), mem_v.reshape(new_shape))
```

```python
import functools
import math

import jax
import jax.numpy as jnp
from jax import lax
from jax.experimental import pallas as pl
from jax.experimental.pallas import tpu as pltpu

EPS = 1e-6
ROPE_THETA = 10000.0
PAGE_SIZE = 128
N_HEADS = 8
QK_NOPE = 128
QK_ROPE = 64
QK_HEAD = QK_NOPE + QK_ROPE
V_HEAD = 128
MEM_HEADS = 4
MEM_HEAD_DIM = 128
D_MEM = MEM_HEADS * MEM_HEAD_DIM
CONV_WIDTH = 3
SM_SCALE = QK_HEAD ** -0.5
MEM_SCALE = MEM_HEAD_DIM ** -0.5

LANES = 128
SUBLANES = 8
HEAD_PAD = 2 * LANES
VMEM_LIMIT = 60000 * 1024
PAGES_PER_STEP = 16
SAMPLE_DMA_SLOTS = 4
SAMPLE_BATCHES_PER_STEP = 16
ATTN_Q_ROWS = 256
HEADS_PER_TRIP = 4
LOG2E = math.log2(math.e)

_F32 = jnp.float32
_BF16 = jnp.bfloat16


def _dot(a, b):
    return jnp.dot(a, b, preferred_element_type=_F32)


def _dot_nt(a, b):
    return lax.dot_general(a, b, (((1,), (1,)), ((), ())), preferred_element_type=_F32)


def _rms(x, g):
    return x * lax.rsqrt(jnp.mean(x * x, axis=-1, keepdims=True) + EPS) * g


def _silu(z):
    return z / (1.0 + jnp.exp(-z))


def _const_spec(shape):
    nd = len(shape)
    return pl.BlockSpec(shape, lambda *_: (0,) * nd, pipeline_mode=pl.Buffered(1))


def _params(n_axes):
    return pltpu.CompilerParams(dimension_semantics=("arbitrary",) * n_axes,
                                vmem_limit_bytes=VMEM_LIMIT)


def _mem_attend(qm, k_ref, v_ref, bi, qg):
    rows = qm.shape[0]
    mem_rows = k_ref.shape[1]
    n_mem = mem_rows // MEM_HEADS
    heads = [slice(h * MEM_HEAD_DIM, (h + 1) * MEM_HEAD_DIM) for h in range(MEM_HEADS)]
    if rows * MEM_HEADS <= LANES:
        q = jnp.concatenate([_rms(qm[:, sl], qg) * (MEM_SCALE * LOG2E) for sl in heads], axis=0).astype(_BF16)
        s = _dot_nt(q, k_ref[bi].astype(_BF16))
        q_head = lax.broadcasted_iota(jnp.int32, s.shape, 0) // rows
        k_head = lax.broadcasted_iota(jnp.int32, s.shape, 1) % MEM_HEADS
        s = jnp.where(q_head == k_head, s, -jnp.inf)
        p = jnp.exp2(s - jnp.max(s, axis=-1, keepdims=True))
        o = _dot(p.astype(_BF16), v_ref[bi].astype(_BF16)) / jnp.sum(p, axis=-1, keepdims=True)
        return [o[h * rows:(h + 1) * rows] for h in range(MEM_HEADS)]
    outs = []
    for sl in heads:
        h = sl.start // MEM_HEAD_DIM
        q = (_rms(qm[:, sl], qg) * (MEM_SCALE * LOG2E)).astype(_BF16)
        k = k_ref[bi, pl.ds(h, n_mem, stride=MEM_HEADS), :].astype(_BF16)
        v = v_ref[bi, pl.ds(h, n_mem, stride=MEM_HEADS), :].astype(_BF16)
        s = _dot_nt(q, k)
        p = jnp.exp2(s - jnp.max(s, axis=-1, keepdims=True))
        o = _dot(p.astype(_BF16), v) / jnp.sum(p, axis=-1, keepdims=True)
        outs.append(o)
    return outs


def _mem_kv_kernel(mem_ref, ng_ref, w_ref, kg_ref, k_out, v_out):
    x = mem_ref[0]
    xn = _rms(x, ng_ref[0]).astype(_BF16)
    kv = _dot(xn, w_ref[0])
    kg = kg_ref[0]
    m = x.shape[0]
    for h in range(MEM_HEADS):
        sl = slice(h * MEM_HEAD_DIM, (h + 1) * MEM_HEAD_DIM)
        rows = pl.ds(h, m, stride=MEM_HEADS)
        k_out[0, 0, rows, :] = _rms(kv[:, sl], kg)
        v_out[0, 0, rows, :] = kv[:, D_MEM + h * MEM_HEAD_DIM:D_MEM + (h + 1) * MEM_HEAD_DIM]


def _mem_kv(mem, norm_g, w_kv, k_g):
    b, m, d = mem.shape
    depth = norm_g.shape[0]
    out = jax.ShapeDtypeStruct((depth, b, m * MEM_HEADS, MEM_HEAD_DIM), _F32)
    return pl.pallas_call(
        _mem_kv_kernel,
        grid=(depth, b),
        in_specs=[
            pl.BlockSpec((1, m, d), lambda l, i: (i, 0, 0)),
            pl.BlockSpec((1, 1, d), lambda l, i: (l, 0, 0)),
            pl.BlockSpec((1, d, 2 * D_MEM), lambda l, i: (l, 0, 0)),
            pl.BlockSpec((1, 1, MEM_HEAD_DIM), lambda l, i: (l, 0, 0)),
        ],
        out_specs=[pl.BlockSpec((1, 1, m * MEM_HEADS, MEM_HEAD_DIM), lambda l, i: (l, i, 0, 0))] * 2,
        out_shape=[out, out],
        compiler_params=_params(2),
        name="mem_kv",
    )(mem, norm_g.reshape(depth, 1, d), w_kv.astype(_BF16), k_g.reshape(depth, 1, MEM_HEAD_DIM))


def _conv_layer_kernel(x_ref, hist_ref, mk_ref, mv_ref, ng_ref, win_ref, cw_ref, wout_ref, qg_ref,
                       y_ref, state_ref, cbuf, obuf, *, nb, tm, dc):
    rows = nb * tm
    d = x_ref.shape[-1]

    @pl.when(pl.program_id(1) == 0)
    def _():
        cbuf[:, SUBLANES - 2:SUBLANES, :] = hist_ref[...]

    x = x_ref[...].reshape(rows, d)
    xn = _rms(x, ng_ref[...]).astype(_BF16)
    proj = _dot(xn, win_ref[...])
    cu = proj[:, 2 * dc:3 * dc] * proj[:, :dc]
    cbuf[:, SUBLANES:SUBLANES + tm, :] = cu.reshape(nb, tm, dc)
    cw = cw_ref[...]
    conv = (cw[0:1] * cbuf[:, SUBLANES - 2:SUBLANES - 2 + tm, :].reshape(rows, dc)
            + cw[1:2] * cbuf[:, SUBLANES - 1:SUBLANES - 1 + tm, :].reshape(rows, dc)
            + cw[2:3] * cu)
    state = cbuf[:, tm + SUBLANES - 2:tm + SUBLANES, :]
    state_ref[...] = state
    cbuf[:, SUBLANES - 2:SUBLANES, :] = state
    zoff = 3 * dc + D_MEM
    obuf[:, :dc] = (proj[:, dc:2 * dc] * conv * _silu(proj[:, zoff:zoff + dc])).astype(_BF16)
    qg = qg_ref[...]
    for bi in range(nb):
        r = slice(bi * tm, (bi + 1) * tm)
        outs = _mem_attend(proj[r, 3 * dc:3 * dc + D_MEM], mk_ref, mv_ref, bi, qg)
        for h, o in enumerate(outs):
            c0 = dc + h * MEM_HEAD_DIM
            obuf[r, c0:c0 + MEM_HEAD_DIM] = (o * _silu(proj[r, zoff + c0:zoff + c0 + MEM_HEAD_DIM])).astype(_BF16)
    y_ref[...] = (x + _dot(obuf[...], wout_ref[...])).reshape(nb, tm, d)


def _conv_layer(x, hist, mem_k, mem_v, mem_layer, mem_q_g, norm_g, w_in, conv_w, w_out, *, nb, tm):
    b, t, d = x.shape
    dc = conv_w.shape[1]
    mem_rows = mem_k.shape[1]
    off = mem_layer * (b // nb)
    kern = functools.partial(_conv_layer_kernel, nb=nb, tm=tm, dc=dc)
    return pl.pallas_call(
        kern,
        grid=(b // nb, t // tm),
        in_specs=[
            pl.BlockSpec((nb, tm, d), lambda i, j: (i, j, 0)),
            pl.BlockSpec((nb, CONV_WIDTH - 1, dc), lambda i, j: (i, 0, 0)),
            pl.BlockSpec((nb, mem_rows, MEM_HEAD_DIM), lambda i, j: (off + i, 0, 0)),
            pl.BlockSpec((nb, mem_rows, MEM_HEAD_DIM), lambda i, j: (off + i, 0, 0)),
            _const_spec((1, d)),
            _const_spec(w_in.shape),
            _const_spec(conv_w.shape),
            _const_spec(w_out.shape),
            _const_spec((1, MEM_HEAD_DIM)),
        ],
        out_specs=[
            pl.BlockSpec((nb, tm, d), lambda i, j: (i, j, 0)),
            pl.BlockSpec((nb, CONV_WIDTH - 1, dc), lambda i, j: (i, 0, 0)),
        ],
        out_shape=[jax.ShapeDtypeStruct((b, t, d), _F32),
                   jax.ShapeDtypeStruct((b, CONV_WIDTH - 1, dc), _F32)],
        scratch_shapes=[pltpu.VMEM((nb, tm + SUBLANES, dc), _F32),
                        pltpu.VMEM((nb * tm, dc + D_MEM), _BF16)],
        compiler_params=_params(2),
        name="conv_layer",
    )(x, hist, mem_k, mem_v, norm_g.reshape(1, d), w_in, conv_w, w_out, mem_q_g.reshape(1, MEM_HEAD_DIM))


def _rope_table(pos):
    inv = 1.0 / (ROPE_THETA ** (jnp.arange(0, QK_ROPE, 2, dtype=_F32) / QK_ROPE))
    ang = pos.astype(_F32)[:, None] * inv[None, :]
    c, s = jnp.cos(ang), jnp.sin(ang)
    return jnp.concatenate([c, c, -s, s], axis=-1)


def _swap_halves(w):
    half = w.shape[-1] // 2
    return jnp.concatenate([w[..., half:], w[..., :half]], axis=-1)


def _mla_weights(mla_w_in, mla_w_uq, mla_q_g, mla_k_g, q_lora, kv_lora):
    o = q_lora + kv_lora
    kpe_w = mla_w_in[:, o:o + QK_ROPE]
    w_in = jnp.concatenate([mla_w_in[:, :o], kpe_w, _swap_halves(kpe_w), mla_w_in[:, o + QK_ROPE:]], axis=1)
    wq = mla_w_uq.reshape(q_lora, N_HEADS, QK_HEAD)
    wq = jnp.concatenate([wq, _swap_halves(wq[..., QK_NOPE:])], axis=-1)
    w_uq = wq.reshape(q_lora, N_HEADS * HEAD_PAD)
    g_rope = mla_q_g[QK_NOPE:]
    q_gain = jnp.concatenate([mla_q_g[:QK_NOPE] * mla_k_g[:QK_NOPE], g_rope, _swap_halves(g_rope)])
    k_rope = mla_k_g[QK_NOPE:]
    k_gain = jnp.concatenate([k_rope, _swap_halves(k_rope)])
    return (w_in.astype(_BF16), w_uq.astype(_BF16), q_gain.reshape(1, HEAD_PAD), k_gain.reshape(1, LANES))


def _rope_lane_mask(rows):
    return lax.broadcasted_iota(jnp.int32, (rows, LANES), 1) < QK_ROPE


def _q_heads(cq, qlg, wuq_ref, q_gain, tab, rope_mask):
    cqn = _rms(cq, qlg).astype(_BF16)
    for h in range(N_HEADS):
        q = _dot(cqn, wuq_ref[:, h * HEAD_PAD:(h + 1) * HEAD_PAD])
        a = q[:, :LANES]
        b = q[:, LANES:]
        ss = (jnp.sum(a * a, axis=-1, keepdims=True)
              + jnp.sum(jnp.where(rope_mask, b * b, 0.0), axis=-1, keepdims=True))
        r = lax.rsqrt(ss / QK_HEAD + EPS)
        yield a * r * q_gain[:, :LANES], b * r * q_gain[:, LANES:] * tab


def _kpe_rot(kpe_blk, k_gain, tab):
    kt = kpe_blk * k_gain * tab
    return kt + pltpu.roll(kt, QK_ROPE, axis=1)


def _head_select(cols, rows):
    lane = lax.broadcasted_iota(jnp.int32, (rows, N_HEADS), 1)
    out = jnp.zeros((rows, N_HEADS), _F32)
    for h, c in enumerate(cols):
        out = jnp.where(lane == h, c, out)
    return out


def _mla_prompt_kernel(x_ref, tab_ref, mk_ref, mv_ref, ng_ref, win_ref, qlg_ref, wuq_ref, qgain_ref,
                       ckvg_ref, wuk_ref, wuv_ref, kgain_ref, wout_ref, qg_ref,
                       y_ref, ckv_ref, kpe_ref, ksc_ref,
                       late, qbuf, kbuf, vbuf, obuf, o3, *, tm, tq, q_lora, kv_lora):
    i = pl.program_id(1)
    x = x_ref[0]
    tab = tab_ref[...]
    rope_mask = _rope_lane_mask(tm)
    xn = _rms(x, ng_ref[...]).astype(_BF16)
    o = q_lora + kv_lora
    early = _dot(xn, win_ref[:, :o + LANES])
    late[...] = _dot(xn, win_ref[:, o + LANES:])

    heads = _q_heads(early[:, :q_lora], qlg_ref[...], wuq_ref, qgain_ref[...] * (SM_SCALE * LOG2E), tab, rope_mask)
    for h, (a, b) in enumerate(heads):
        qbuf[h, :, :LANES] = a.astype(_BF16)
        qbuf[h, :, LANES:] = b.astype(_BF16)

    ckv = _rms(early[:, q_lora:o], ckvg_ref[...])
    ckv_ref[0] = ckv
    ckv_b = ckv.astype(_BF16)
    kpe_blk = early[:, o:o + LANES]
    skpe = jnp.sum(jnp.where(rope_mask, kpe_blk * kpe_blk, 0.0), axis=-1, keepdims=True)
    rot = _kpe_rot(kpe_blk, kgain_ref[...], tab)
    kpe_ref[0] = rot[:, :QK_ROPE]
    row0 = pl.multiple_of(i * tm, tm)
    kscs = []
    for h2 in range(0, N_HEADS, 2):
        k_pair = _dot(ckv_b, wuk_ref[:, h2 * QK_NOPE:(h2 + 2) * QK_NOPE])
        v_pair = _dot(ckv_b, wuv_ref[:, h2 * V_HEAD:(h2 + 2) * V_HEAD])
        for h in (h2, h2 + 1):
            kn = k_pair[:, (h - h2) * QK_NOPE:(h - h2 + 1) * QK_NOPE]
            ksc = lax.rsqrt((jnp.sum(kn * kn, axis=-1, keepdims=True) + skpe) / QK_HEAD + EPS)
            kscs.append(ksc)
            kbuf[h, pl.ds(row0, tm), :LANES] = (kn * ksc).astype(_BF16)
            kbuf[h, pl.ds(row0, tm), LANES:] = (rot * ksc).astype(_BF16)
            vbuf[h, pl.ds(row0, tm), :V_HEAD] = v_pair[:, (h - h2) * V_HEAD:(h - h2 + 1) * V_HEAD].astype(_BF16)
            vbuf[h, pl.ds(row0, tm), V_HEAD:] = jnp.ones((tm, LANES), _BF16)
    ksc_ref[0] = _head_select(kscs, tm)

    causal = (lax.broadcasted_iota(jnp.int32, (tq, tq), 0) >= lax.broadcasted_iota(jnp.int32, (tq, tq), 1))

    def attend_block(h, r0, n_keys):
        q = qbuf[h, r0:r0 + tq, :]
        s = _dot_nt(q, kbuf[h, 0:n_keys, :])
        diag = jnp.where(causal, s[:, n_keys - tq:], -jnp.inf)
        m = jnp.max(diag, axis=-1, keepdims=True)
        if n_keys > tq:
            past = s[:, :n_keys - tq]
            m = jnp.maximum(m, jnp.max(past, axis=-1, keepdims=True))
            p = jnp.concatenate([jnp.exp2(past - m), jnp.exp2(diag - m)], axis=1)
        else:
            p = jnp.exp2(diag - m)
        ov = _dot(p.astype(_BF16), vbuf[h, 0:n_keys, :])
        o3[h, r0:r0 + tq, :] = (ov[:, :V_HEAD] / ov[:, V_HEAD:]).astype(o3.dtype)

    for ti in range(kbuf.shape[1] // tm):
        @pl.when(i == ti)
        def _(ti=ti):
            def head_body(hp, carry):
                for r0 in range(0, tm, tq):
                    for dh in range(HEADS_PER_TRIP):
                        attend_block(HEADS_PER_TRIP * hp + dh, r0, ti * tm + r0 + tq)
                return carry

            lax.fori_loop(0, N_HEADS // HEADS_PER_TRIP, head_body, 0)

    d_mla = N_HEADS * V_HEAD
    zoff = D_MEM
    for h in range(N_HEADS):
        c0 = h * V_HEAD
        obuf[:, c0:c0 + V_HEAD] = (o3[h].astype(_F32) * _silu(late[:, zoff + c0:zoff + c0 + V_HEAD])).astype(_BF16)
    outs = _mem_attend(late[:, :D_MEM], mk_ref, mv_ref, 0, qg_ref[...])
    for h, om in enumerate(outs):
        c0 = d_mla + h * MEM_HEAD_DIM
        obuf[:, c0:c0 + MEM_HEAD_DIM] = (om * _silu(late[:, zoff + c0:zoff + c0 + MEM_HEAD_DIM])).astype(_BF16)
    y_ref[0] = x + _dot(obuf[...], wout_ref[...])


def _mla_prompt(x, tab, mem_k, mem_v, mem_layer, mem_q_g, norm_g, w_in, q_lora_g, w_uq, q_gain, ckv_g,
                w_uk, w_uv, k_gain, w_out, *, tm):
    b, t, d = x.shape
    q_lora = q_lora_g.shape[0]
    kv_lora = ckv_g.shape[0]
    mem_rows = mem_k.shape[1]
    off = mem_layer * b
    d_out = N_HEADS * V_HEAD + D_MEM
    tq = math.gcd(tm, ATTN_Q_ROWS)
    kern = functools.partial(_mla_prompt_kernel, tm=tm, tq=tq, q_lora=q_lora, kv_lora=kv_lora)
    tok = lambda w: pl.BlockSpec((1, tm, w), lambda i, j: (i, j, 0))
    return pl.pallas_call(
        kern,
        grid=(b, t // tm),
        in_specs=[
            tok(d),
            pl.BlockSpec((tm, LANES), lambda i, j: (j, 0)),
            pl.BlockSpec((1, mem_rows, MEM_HEAD_DIM), lambda i, j: (off + i, 0, 0), pipeline_mode=pl.Buffered(1)),
            pl.BlockSpec((1, mem_rows, MEM_HEAD_DIM), lambda i, j: (off + i, 0, 0), pipeline_mode=pl.Buffered(1)),
            _const_spec((1, d)),
            _const_spec(w_in.shape),
            _const_spec((1, q_lora)),
            _const_spec(w_uq.shape),
            _const_spec((1, HEAD_PAD)),
            _const_spec((1, kv_lora)),
            _const_spec(w_uk.shape),
            _const_spec(w_uv.shape),
            _const_spec((1, LANES)),
            _const_spec(w_out.shape),
            _const_spec((1, MEM_HEAD_DIM)),
        ],
        out_specs=[tok(d), tok(kv_lora), tok(QK_ROPE), tok(N_HEADS)],
        out_shape=[jax.ShapeDtypeStruct((b, t, d), _F32),
                   jax.ShapeDtypeStruct((b, t, kv_lora), _F32),
                   jax.ShapeDtypeStruct((b, t, QK_ROPE), _F32),
                   jax.ShapeDtypeStruct((b, t, N_HEADS), _F32)],
        scratch_shapes=[
            pltpu.VMEM((tm, D_MEM + d_out), _F32),
            pltpu.VMEM((N_HEADS, tm, HEAD_PAD), _BF16),
            pltpu.VMEM((N_HEADS, t, HEAD_PAD), _BF16),
            pltpu.VMEM((N_HEADS, t, V_HEAD + LANES), _BF16),
            pltpu.VMEM((tm, d_out), _BF16),
            pltpu.VMEM((N_HEADS, tm, V_HEAD), _BF16),
        ],
        compiler_params=_params(2),
        name="mla_prompt",
    )(x, tab, mem_k, mem_v, norm_g.reshape(1, d), w_in, q_lora_g.reshape(1, q_lora), w_uq, q_gain,
      ckv_g.reshape(1, kv_lora), w_uk, w_uv, k_gain, w_out, mem_q_g.reshape(1, MEM_HEAD_DIM))


def _mla_sample_proj_kernel(x_ref, tab_ref, ng_ref, win_ref, qlg_ref, wuq_ref, qgain_ref, ckvg_ref,
                            wuk_ref, wukt_ref, kgain_ref,
                            qabs_ref, qpe_ref, ckv_ref, kpe_ref, ksc_ref, rest_ref, *, q_lora, kv_lora):
    rows = x_ref.shape[0]
    tab = tab_ref[...]
    rope_mask = _rope_lane_mask(rows)
    xn = _rms(x_ref[...], ng_ref[...]).astype(_BF16)
    proj = _dot(xn, win_ref[...])
    o = q_lora + kv_lora
    rest_ref[...] = proj[:, o + LANES:]

    heads = _q_heads(proj[:, :q_lora], qlg_ref[...], wuq_ref, qgain_ref[...] * (SM_SCALE * LOG2E), tab, rope_mask)
    for h, (a, b) in enumerate(heads):
        qabs_ref[:, h * kv_lora:(h + 1) * kv_lora] = _dot(a.astype(_BF16), wukt_ref[h]).astype(_BF16)
        qpe_ref[:, h * LANES:(h + 1) * LANES] = (b + pltpu.roll(b, QK_ROPE, axis=1)).astype(_BF16)

    ckv = _rms(proj[:, q_lora:o], ckvg_ref[...])
    ckv_ref[...] = ckv
    k_nope = _dot(ckv.astype(_BF16), wuk_ref[...])
    kpe_blk = proj[:, o:o + LANES]
    skpe = jnp.sum(jnp.where(rope_mask, kpe_blk * kpe_blk, 0.0), axis=-1, keepdims=True)
    kpe_ref[...] = _kpe_rot(kpe_blk, kgain_ref[...], tab)[:, :QK_ROPE]
    kscs = []
    for h in range(N_HEADS):
        kn = k_nope[:, h * QK_NOPE:(h + 1) * QK_NOPE]
        kscs.append(lax.rsqrt((jnp.sum(kn * kn, axis=-1, keepdims=True) + skpe) / QK_HEAD + EPS))
    ksc_ref[...] = _head_select(kscs, rows)


def _mla_sample_proj(x2, tab, norm_g, w_in, q_lora_g, w_uq, q_gain, ckv_g, w_uk, w_ukt, k_gain, *, tm):
    n, d = x2.shape
    q_lora = q_lora_g.shape[0]
    kv_lora = ckv_g.shape[0]
    n_rest = w_in.shape[1] - (q_lora + kv_lora + LANES)
    kern = functools.partial(_mla_sample_proj_kernel, q_lora=q_lora, kv_lora=kv_lora)
    tok = lambda w: pl.BlockSpec((tm, w), lambda i: (i, 0))
    return pl.pallas_call(
        kern,
        grid=(n // tm,),
        in_specs=[
            tok(d), tok(LANES),
            _const_spec((1, d)),
            _const_spec(w_in.shape),
            _const_spec((1, q_lora)),
            _const_spec(w_uq.shape),
            _const_spec((1, HEAD_PAD)),
            _const_spec((1, kv_lora)),
            _const_spec(w_uk.shape),
            _const_spec(w_ukt.shape),
            _const_spec((1, LANES)),
        ],
        out_specs=[tok(N_HEADS * kv_lora), tok(N_HEADS * LANES), tok(kv_lora), tok(QK_ROPE), tok(N_HEADS),
                   tok(n_rest)],
        out_shape=[jax.ShapeDtypeStruct((n, N_HEADS * kv_lora), _BF16),
                   jax.ShapeDtypeStruct((n, N_HEADS * LANES), _BF16),
                   jax.ShapeDtypeStruct((n, kv_lora), _F32),
                   jax.ShapeDtypeStruct((n, QK_ROPE), _F32),
                   jax.ShapeDtypeStruct((n, N_HEADS), _F32),
                   jax.ShapeDtypeStruct((n, n_rest), _F32)],
        compiler_params=_params(1),
        name="mla_sample_proj",
    )(x2, tab, norm_g.reshape(1, d), w_in, q_lora_g.reshape(1, q_lora), w_uq, q_gain,
      ckv_g.reshape(1, kv_lora), w_uk, w_ukt, k_gain)


def _mla_sample_attn_kernel(pt_ref, qabs_ref, qpe_ref, nckv_ref, nkpe_ref, nksc_ref, ckv_hbm, kpe_hbm, ksc_hbm,
                            o_ref, ckv_buf, kpe_buf, ksc_buf, sems, ck_all, s_all, *, pps, n_blocks, n_slots, dec):
    b = pl.program_id(0)
    rows = N_HEADS * dec
    blk_keys = pps * PAGE_SIZE
    n_keys = n_blocks * blk_keys
    ahead = n_slots - 1

    def page_copies(bb, blk, slot):
        copies = []
        for p in range(pps):
            page = pt_ref[bb, blk * pps + p]
            keys = pl.ds(p * PAGE_SIZE, PAGE_SIZE)
            copies.append(pltpu.make_async_copy(ckv_hbm.at[page], ckv_buf.at[slot, keys, :], sems.at[0, slot]))
            copies.append(pltpu.make_async_copy(kpe_hbm.at[page], kpe_buf.at[slot, p], sems.at[1, slot]))
            copies.append(pltpu.make_async_copy(ksc_hbm.at[page], ksc_buf.at[slot, p], sems.at[2, slot]))
        return copies

    def slot_of(rel):
        return rel % n_slots if n_blocks % n_slots == 0 else lax.rem(b * n_blocks + rel, n_slots)

    def start_block(rel):
        db, blk = divmod(rel, n_blocks)

        def go():
            for c in page_copies(b + db, blk, slot_of(rel)):
                c.start()

        if db == 0:
            go()
        else:
            pl.when(b + db < pl.num_programs(0))(go)

    @pl.when(b == 0)
    def _():
        for rel in range(ahead):
            start_block(rel)

    qa = qabs_ref[0]
    qp = qpe_ref[0][:, :QK_ROPE]

    mx = jnp.full((rows, LANES), -jnp.inf, _F32)
    for blk in range(n_blocks):
        start_block(blk + ahead)
        slot = slot_of(blk)
        for c in page_copies(b, blk, slot):
            c.wait()
        ck = ckv_buf[slot].astype(_BF16)
        ck_all[blk * blk_keys:(blk + 1) * blk_keys, :] = ck
        kpt = jnp.concatenate([kpe_buf[slot, p] for p in range(pps)], axis=1).astype(_BF16)
        kst = jnp.concatenate([ksc_buf[slot, p] for p in range(pps)], axis=1)
        s = _dot_nt(qa, ck) + _dot(qp, kpt)
        s = (s.reshape(dec, N_HEADS, blk_keys) * kst[None]).reshape(rows, blk_keys)
        s_all[:, blk * blk_keys:(blk + 1) * blk_keys] = s
        for t in range(blk_keys // LANES):
            mx = jnp.maximum(mx, s[:, t * LANES:(t + 1) * LANES])

    nckv = nckv_ref[0].astype(_BF16)
    rr = lax.broadcasted_iota(jnp.int32, (rows, N_HEADS), 0)
    hh = lax.broadcasted_iota(jnp.int32, (rows, N_HEADS), 1)
    expand = (rr % N_HEADS == hh).astype(_F32)
    scale = lax.dot_general(expand, nksc_ref[0], (((1,), (1,)), ((), ())), precision=lax.Precision.HIGHEST,
                            preferred_element_type=_F32)
    s_new = (_dot_nt(qa, nckv) + _dot_nt(qp, nkpe_ref[0].astype(_BF16))) * scale
    tq = lax.broadcasted_iota(jnp.int32, (rows, dec), 0) // N_HEADS
    tk = lax.broadcasted_iota(jnp.int32, (rows, dec), 1)
    s_new = jnp.where(tk <= tq, s_new, -jnp.inf)

    m = jnp.maximum(jnp.max(mx, axis=-1, keepdims=True), jnp.max(s_new, axis=-1, keepdims=True))
    p_new = jnp.exp2(s_new - m)
    p = jnp.exp2(s_all[...] - m)
    l = jnp.sum(p, axis=-1, keepdims=True) + jnp.sum(p_new, axis=-1, keepdims=True)
    acc = _dot(p.astype(_BF16), ck_all[...]) + _dot(p_new.astype(_BF16), nckv)
    o_ref[0] = acc / l


def _mla_sample_attn(page_table, qabs, qpe, nckv, nkpe, nksc, cache_ckv, cache_kpe_t, cache_ksc_t):
    b, rows, kv_lora = qabs.shape
    dec = nckv.shape[1]
    n_pages = page_table.shape[1]
    pps = math.gcd(n_pages, PAGES_PER_STEP)
    n_blocks = n_pages // pps
    blk_keys = pps * PAGE_SIZE
    n_keys = n_pages * PAGE_SIZE
    n_slots = SAMPLE_DMA_SLOTS
    kern = functools.partial(_mla_sample_attn_kernel, pps=pps, n_blocks=n_blocks, n_slots=n_slots, dec=dec)
    per_b = lambda shape: pl.BlockSpec((1,) + shape, lambda i, pt: (i, 0, 0))
    hbm = pl.BlockSpec(memory_space=pl.ANY)
    grid_spec = pltpu.PrefetchScalarGridSpec(
        num_scalar_prefetch=1,
        grid=(b,),
        in_specs=[per_b((rows, kv_lora)), per_b((rows, LANES)), per_b((dec, kv_lora)), per_b((dec, QK_ROPE)),
                  per_b((dec, N_HEADS)), hbm, hbm, hbm],
        out_specs=per_b((rows, kv_lora)),
        scratch_shapes=[
            pltpu.VMEM((n_slots, blk_keys, kv_lora), _F32),
            pltpu.VMEM((n_slots, pps, QK_ROPE, PAGE_SIZE), _F32),
            pltpu.VMEM((n_slots, pps, N_HEADS, PAGE_SIZE), _F32),
            pltpu.SemaphoreType.DMA((3, n_slots)),
            pltpu.VMEM((n_keys, kv_lora), _BF16),
            pltpu.VMEM((rows, n_keys), _F32),
        ],
    )
    return pl.pallas_call(
        kern,
        grid_spec=grid_spec,
        out_shape=jax.ShapeDtypeStruct((b, rows, kv_lora), _F32),
        compiler_params=_params(1),
        name="mla_sample_attn",
    )(page_table, qabs, qpe, nckv, nkpe, nksc, cache_ckv, cache_kpe_t, cache_ksc_t)


def _mla_sample_out_kernel(x_ref, olat_ref, rest_ref, mk_ref, mv_ref, wuv_ref, wout_ref, qg_ref,
                           y_ref, obuf, *, nb, dec, kv_lora):
    d_mla = N_HEADS * V_HEAD
    zoff = D_MEM
    for h in range(N_HEADS):
        c0 = h * V_HEAD
        y_h = _dot(olat_ref[:, h * kv_lora:(h + 1) * kv_lora].astype(_BF16), wuv_ref[h])
        obuf[:, c0:c0 + V_HEAD] = (y_h * _silu(rest_ref[:, zoff + c0:zoff + c0 + V_HEAD])).astype(_BF16)
    qg = qg_ref[...]
    for bi in range(nb):
        r = slice(bi * dec, (bi + 1) * dec)
        outs = _mem_attend(rest_ref[r, :D_MEM], mk_ref, mv_ref, bi, qg)
        for h, om in enumerate(outs):
            c0 = d_mla + h * MEM_HEAD_DIM
            obuf[r, c0:c0 + MEM_HEAD_DIM] = (om * _silu(rest_ref[r, zoff + c0:zoff + c0 + MEM_HEAD_DIM])).astype(_BF16)
    y_ref[...] = x_ref[...] + _dot(obuf[...], wout_ref[...])


def _mla_sample_out(x2, olat, rest, mem_k, mem_v, mem_layer, mem_q_g, w_uv3, w_out, *, nb, dec):
    n, d = x2.shape
    kv_lora = w_uv3.shape[1]
    mem_rows = mem_k.shape[1]
    rows = nb * dec
    off = mem_layer * (n // rows)
    kern = functools.partial(_mla_sample_out_kernel, nb=nb, dec=dec, kv_lora=kv_lora)
    tok = lambda w: pl.BlockSpec((rows, w), lambda i: (i, 0))
    return pl.pallas_call(
        kern,
        grid=(n // rows,),
        in_specs=[
            tok(d), tok(olat.shape[1]), tok(rest.shape[1]),
            pl.BlockSpec((nb, mem_rows, MEM_HEAD_DIM), lambda i: (off + i, 0, 0)),
            pl.BlockSpec((nb, mem_rows, MEM_HEAD_DIM), lambda i: (off + i, 0, 0)),
            _const_spec(w_uv3.shape),
            _const_spec(w_out.shape),
            _const_spec((1, MEM_HEAD_DIM)),
        ],
        out_specs=tok(d),
        out_shape=jax.ShapeDtypeStruct((n, d), _F32),
        scratch_shapes=[pltpu.VMEM((rows, w_out.shape[0]), _BF16)],
        compiler_params=_params(1),
        name="mla_sample_out",
    )(x2, olat, rest, mem_k, mem_v, w_uv3, w_out, mem_q_g.reshape(1, MEM_HEAD_DIM))


def _pick_tile(n, target):
    t = min(n, target)
    while n % t or t % SUBLANES:
        t -= 1
    return t


def kernel(x_prompt, x_sample, mem_prompt, state_conv, cache_ckv, cache_kpe, cache_kscale, cache_mem_k, cache_mem_v, page_table, conv_norm_g, conv_w_in, conv_w, conv_w_out, mla_norm_g, mla_w_in, mla_q_lora_g, mla_w_uq, mla_ckv_g, mla_w_uk, mla_w_uv, mla_q_g, mla_k_g, mla_w_out, mem_norm_g, mem_w_kv, mem_q_g, mem_k_g):
    bp, seq, d = x_prompt.shape
    bs, dec, _ = x_sample.shape
    depth = mem_norm_g.shape[0]
    n_mem = mem_prompt.shape[1]
    dc = conv_w.shape[1]
    q_lora = mla_q_lora_g.shape[0]
    kv_lora = mla_ckv_g.shape[0]
    past = page_table.shape[1] * PAGE_SIZE

    tm_p = _pick_tile(seq, 512)
    nb_s = _pick_tile(bs, SAMPLE_BATCHES_PER_STEP) if bs % SUBLANES == 0 else bs

    mem_k, mem_v = _mem_kv(mem_prompt, mem_norm_g, mem_w_kv, mem_k_g)
    mem_rows = n_mem * MEM_HEADS
    mem_k2 = mem_k.reshape(depth * bp, mem_rows, MEM_HEAD_DIM)
    mem_v2 = mem_v.reshape(depth * bp, mem_rows, MEM_HEAD_DIM)
    smem_k2 = cache_mem_k.reshape(depth * bs, mem_rows, MEM_HEAD_DIM)
    smem_v2 = cache_mem_v.reshape(depth * bs, mem_rows, MEM_HEAD_DIM)

    cw_in = conv_w_in.astype(_BF16)
    cw_out = conv_w_out.astype(_BF16)
    hist_p = jnp.zeros((bp, CONV_WIDTH - 1, dc), _F32)
    y_p, conv_p = _conv_layer(x_prompt, hist_p, mem_k2, mem_v2, 0, mem_q_g[0], conv_norm_g, cw_in, conv_w, cw_out,
                              nb=1, tm=tm_p)
    y_s, conv_s = _conv_layer(x_sample, state_conv, smem_k2, smem_v2, 0, mem_q_g[0], conv_norm_g, cw_in, conv_w,
                              cw_out, nb=nb_s, tm=dec)

    w_in, w_uq, q_gain, k_gain = _mla_weights(mla_w_in, mla_w_uq, mla_q_g, mla_k_g, q_lora, kv_lora)
    w_uk2 = mla_w_uk.reshape(kv_lora, N_HEADS * QK_NOPE).astype(_BF16)
    w_uv2 = mla_w_uv.reshape(kv_lora, N_HEADS * V_HEAD).astype(_BF16)
    w_out = mla_w_out.astype(_BF16)
    tab_p = _rope_table(jnp.arange(seq, dtype=jnp.int32))
    y_p, ckv_p, kpe_p, ksc_p = _mla_prompt(y_p, tab_p, mem_k2, mem_v2, 1, mem_q_g[1], mla_norm_g, w_in,
                                           mla_q_lora_g, w_uq, q_gain, mla_ckv_g, w_uk2, w_uv2, k_gain, w_out,
                                           tm=tm_p)

    n_s = bs * dec
    tab_s = jnp.tile(_rope_table(past + jnp.arange(dec, dtype=jnp.int32)), (bs, 1))
    w_ukt = jnp.transpose(mla_w_uk, (1, 2, 0)).astype(_BF16)
    w_uv3 = jnp.transpose(mla_w_uv, (1, 0, 2)).astype(_BF16)
    qabs, qpe, ckv_s, kpe_s, ksc_s, rest = _mla_sample_proj(
        y_s.reshape(n_s, d), tab_s, mla_norm_g, w_in, mla_q_lora_g, w_uq, q_gain, mla_ckv_g, w_uk2, w_ukt, k_gain,
        tm=_pick_tile(n_s, 512))
    rows = dec * N_HEADS
    olat = _mla_sample_attn(page_table, qabs.reshape(bs, rows, kv_lora), qpe.reshape(bs, rows, LANES),
                            ckv_s.reshape(bs, dec, kv_lora), kpe_s.reshape(bs, dec, QK_ROPE),
                            ksc_s.reshape(bs, dec, N_HEADS), cache_ckv, jnp.transpose(cache_kpe, (0, 2, 1)),
                            jnp.transpose(cache_kscale, (0, 2, 1)))
    y_s2 = _mla_sample_out(y_s.reshape(n_s, d), olat.reshape(n_s, N_HEADS * kv_lora), rest, smem_k2, smem_v2, 1,
                           mem_q_g[1], w_uv3, w_out, nb=nb_s, dec=dec)

    new_shape = (depth, bp, n_mem, MEM_HEADS, MEM_HEAD_DIM)
    return (y_p, y_s2.reshape(bs, dec, d), conv_p, conv_s, ckv_p, kpe_p, ksc_p,
            ckv_s.reshape(bs, dec, kv_lora), kpe_s.reshape(bs, dec, QK_ROPE), ksc_s.reshape(bs, dec, N_HEADS),
            mem_k.reshape(new_shape), mem_v.reshape(new_shape))
```

```python
import functools
import math

import jax
import jax.numpy as jnp
from jax import lax
from jax.experimental import pallas as pl
from jax.experimental.pallas import tpu as pltpu

EPS = 1e-6
ROPE_THETA = 10000.0
PAGE_SIZE = 128
N_HEADS = 8
QK_NOPE = 128
QK_ROPE = 64
QK_HEAD = QK_NOPE + QK_ROPE
V_HEAD = 128
MEM_HEADS = 4
MEM_HEAD_DIM = 128
D_MEM = MEM_HEADS * MEM_HEAD_DIM
CONV_WIDTH = 3
SM_SCALE = QK_HEAD ** -0.5
MEM_SCALE = MEM_HEAD_DIM ** -0.5

LANES = 128
SUBLANES = 8
HEAD_PAD = 2 * LANES
VMEM_LIMIT = 60000 * 1024
PAGES_PER_STEP = 16
SAMPLE_DMA_SLOTS = 4
SAMPLE_BATCHES_PER_STEP = 16
ATTN_Q_ROWS = 256
HEADS_PER_TRIP = 8
LOG2E = math.log2(math.e)

_F32 = jnp.float32
_BF16 = jnp.bfloat16


def _dot(a, b):
    return jnp.dot(a, b, preferred_element_type=_F32)


def _dot_nt(a, b):
    return lax.dot_general(a, b, (((1,), (1,)), ((), ())), preferred_element_type=_F32)


def _rms(x, g):
    return x * lax.rsqrt(jnp.mean(x * x, axis=-1, keepdims=True) + EPS) * g


def _silu(z):
    h = 0.5 * z
    return h + h * jnp.tanh(h)


def _const_spec(shape):
    nd = len(shape)
    return pl.BlockSpec(shape, lambda *_: (0,) * nd, pipeline_mode=pl.Buffered(1))


def _params(n_axes):
    return pltpu.CompilerParams(dimension_semantics=("arbitrary",) * n_axes,
                                vmem_limit_bytes=VMEM_LIMIT)


def _mem_attend(qm, k_ref, v_ref, bi, qg):
    rows = qm.shape[0]
    mem_rows = k_ref.shape[1]
    n_mem = mem_rows // MEM_HEADS
    heads = [slice(h * MEM_HEAD_DIM, (h + 1) * MEM_HEAD_DIM) for h in range(MEM_HEADS)]
    if rows * MEM_HEADS <= LANES:
        q = jnp.concatenate([_rms(qm[:, sl], qg) * (MEM_SCALE * LOG2E) for sl in heads], axis=0).astype(_BF16)
        s = _dot_nt(q, k_ref[bi].astype(_BF16))
        q_head = lax.broadcasted_iota(jnp.int32, s.shape, 0) // rows
        k_head = lax.broadcasted_iota(jnp.int32, s.shape, 1) % MEM_HEADS
        s = jnp.where(q_head == k_head, s, -jnp.inf)
        p = jnp.exp2(s - jnp.max(s, axis=-1, keepdims=True))
        o = _dot(p.astype(_BF16), v_ref[bi].astype(_BF16)) / jnp.sum(p, axis=-1, keepdims=True)
        return [o[h * rows:(h + 1) * rows] for h in range(MEM_HEADS)]
    outs = []
    for sl in heads:
        h = sl.start // MEM_HEAD_DIM
        q = (_rms(qm[:, sl], qg) * (MEM_SCALE * LOG2E)).astype(_BF16)
        k = k_ref[bi, pl.ds(h, n_mem, stride=MEM_HEADS), :].astype(_BF16)
        v = v_ref[bi, pl.ds(h, n_mem, stride=MEM_HEADS), :].astype(_BF16)
        s = _dot_nt(q, k)
        p = jnp.exp2(s - jnp.max(s, axis=-1, keepdims=True))
        o = _dot(p.astype(_BF16), v) / jnp.sum(p, axis=-1, keepdims=True)
        outs.append(o)
    return outs


def _mem_kv_kernel(mem_ref, ng_ref, w_ref, kg_ref, k_out, v_out):
    x = mem_ref[0]
    xn = _rms(x, ng_ref[0]).astype(_BF16)
    kv = _dot(xn, w_ref[0])
    kg = kg_ref[0]
    m = x.shape[0]
    for h in range(MEM_HEADS):
        sl = slice(h * MEM_HEAD_DIM, (h + 1) * MEM_HEAD_DIM)
        rows = pl.ds(h, m, stride=MEM_HEADS)
        k_out[0, 0, rows, :] = _rms(kv[:, sl], kg)
        v_out[0, 0, rows, :] = kv[:, D_MEM + h * MEM_HEAD_DIM:D_MEM + (h + 1) * MEM_HEAD_DIM]


def _mem_kv(mem, norm_g, w_kv, k_g):
    b, m, d = mem.shape
    depth = norm_g.shape[0]
    out = jax.ShapeDtypeStruct((depth, b, m * MEM_HEADS, MEM_HEAD_DIM), _F32)
    return pl.pallas_call(
        _mem_kv_kernel,
        grid=(depth, b),
        in_specs=[
            pl.BlockSpec((1, m, d), lambda l, i: (i, 0, 0)),
            pl.BlockSpec((1, 1, d), lambda l, i: (l, 0, 0)),
            pl.BlockSpec((1, d, 2 * D_MEM), lambda l, i: (l, 0, 0)),
            pl.BlockSpec((1, 1, MEM_HEAD_DIM), lambda l, i: (l, 0, 0)),
        ],
        out_specs=[pl.BlockSpec((1, 1, m * MEM_HEADS, MEM_HEAD_DIM), lambda l, i: (l, i, 0, 0))] * 2,
        out_shape=[out, out],
        compiler_params=_params(2),
        name="mem_kv",
    )(mem, norm_g.reshape(depth, 1, d), w_kv.astype(_BF16), k_g.reshape(depth, 1, MEM_HEAD_DIM))


def _conv_layer_kernel(x_ref, hist_ref, mk_ref, mv_ref, ng_ref, win_ref, cw_ref, wout_ref, qg_ref,
                       y_ref, state_ref, cbuf, obuf, *, nb, tm, dc):
    rows = nb * tm
    d = x_ref.shape[-1]

    @pl.when(pl.program_id(1) == 0)
    def _():
        cbuf[:, SUBLANES - 2:SUBLANES, :] = hist_ref[...]

    x = x_ref[...].reshape(rows, d)
    xn = _rms(x, ng_ref[...]).astype(_BF16)
    proj = _dot(xn, win_ref[...])
    cu = proj[:, 2 * dc:3 * dc] * proj[:, :dc]
    cbuf[:, SUBLANES:SUBLANES + tm, :] = cu.reshape(nb, tm, dc)
    cw = cw_ref[...]
    conv = (cw[0:1] * cbuf[:, SUBLANES - 2:SUBLANES - 2 + tm, :].reshape(rows, dc)
            + cw[1:2] * cbuf[:, SUBLANES - 1:SUBLANES - 1 + tm, :].reshape(rows, dc)
            + cw[2:3] * cu)
    state = cbuf[:, tm + SUBLANES - 2:tm + SUBLANES, :]
    state_ref[...] = state
    cbuf[:, SUBLANES - 2:SUBLANES, :] = state
    zoff = 3 * dc + D_MEM
    obuf[:, :dc] = (proj[:, dc:2 * dc] * conv * _silu(proj[:, zoff:zoff + dc])).astype(_BF16)
    qg = qg_ref[...]
    for bi in range(nb):
        r = slice(bi * tm, (bi + 1) * tm)
        outs = _mem_attend(proj[r, 3 * dc:3 * dc + D_MEM], mk_ref, mv_ref, bi, qg)
        for h, o in enumerate(outs):
            c0 = dc + h * MEM_HEAD_DIM
            obuf[r, c0:c0 + MEM_HEAD_DIM] = (o * _silu(proj[r, zoff + c0:zoff + c0 + MEM_HEAD_DIM])).astype(_BF16)
    y_ref[...] = (x + _dot(obuf[...], wout_ref[...])).reshape(nb, tm, d)


def _conv_layer(x, hist, mem_k, mem_v, mem_layer, mem_q_g, norm_g, w_in, conv_w, w_out, *, nb, tm):
    b, t, d = x.shape
    dc = conv_w.shape[1]
    mem_rows = mem_k.shape[1]
    off = mem_layer * (b // nb)
    kern = functools.partial(_conv_layer_kernel, nb=nb, tm=tm, dc=dc)
    return pl.pallas_call(
        kern,
        grid=(b // nb, t // tm),
        in_specs=[
            pl.BlockSpec((nb, tm, d), lambda i, j: (i, j, 0)),
            pl.BlockSpec((nb, CONV_WIDTH - 1, dc), lambda i, j: (i, 0, 0)),
            pl.BlockSpec((nb, mem_rows, MEM_HEAD_DIM), lambda i, j: (off + i, 0, 0)),
            pl.BlockSpec((nb, mem_rows, MEM_HEAD_DIM), lambda i, j: (off + i, 0, 0)),
            _const_spec((1, d)),
            _const_spec(w_in.shape),
            _const_spec(conv_w.shape),
            _const_spec(w_out.shape),
            _const_spec((1, MEM_HEAD_DIM)),
        ],
        out_specs=[
            pl.BlockSpec((nb, tm, d), lambda i, j: (i, j, 0)),
            pl.BlockSpec((nb, CONV_WIDTH - 1, dc), lambda i, j: (i, 0, 0)),
        ],
        out_shape=[jax.ShapeDtypeStruct((b, t, d), _F32),
                   jax.ShapeDtypeStruct((b, CONV_WIDTH - 1, dc), _F32)],
        scratch_shapes=[pltpu.VMEM((nb, tm + SUBLANES, dc), _F32),
                        pltpu.VMEM((nb * tm, dc + D_MEM), _BF16)],
        compiler_params=_params(2),
        name="conv_layer",
    )(x, hist, mem_k, mem_v, norm_g.reshape(1, d), w_in, conv_w, w_out, mem_q_g.reshape(1, MEM_HEAD_DIM))


def _rope_table(pos):
    inv = 1.0 / (ROPE_THETA ** (jnp.arange(0, QK_ROPE, 2, dtype=_F32) / QK_ROPE))
    ang = pos.astype(_F32)[:, None] * inv[None, :]
    c, s = jnp.cos(ang), jnp.sin(ang)
    return jnp.concatenate([c, c, -s, s], axis=-1)


def _swap_halves(w):
    half = w.shape[-1] // 2
    return jnp.concatenate([w[..., half:], w[..., :half]], axis=-1)


def _mla_weights(mla_w_in, mla_w_uq, mla_q_g, mla_k_g, q_lora, kv_lora):
    o = q_lora + kv_lora
    kpe_w = mla_w_in[:, o:o + QK_ROPE]
    w_in = jnp.concatenate([mla_w_in[:, :o], kpe_w, _swap_halves(kpe_w), mla_w_in[:, o + QK_ROPE:]], axis=1)
    wq = mla_w_uq.reshape(q_lora, N_HEADS, QK_HEAD)
    wq = jnp.concatenate([wq, _swap_halves(wq[..., QK_NOPE:])], axis=-1)
    w_uq = wq.reshape(q_lora, N_HEADS * HEAD_PAD)
    g_rope = mla_q_g[QK_NOPE:]
    q_gain = jnp.concatenate([mla_q_g[:QK_NOPE] * mla_k_g[:QK_NOPE], g_rope, _swap_halves(g_rope)])
    k_rope = mla_k_g[QK_NOPE:]
    k_gain = jnp.concatenate([k_rope, _swap_halves(k_rope)])
    return (w_in.astype(_BF16), w_uq.astype(_BF16), q_gain.reshape(1, HEAD_PAD), k_gain.reshape(1, LANES))


def _rope_lane_mask(rows):
    return lax.broadcasted_iota(jnp.int32, (rows, LANES), 1) < QK_ROPE


def _q_heads(cq, qlg, wuq_ref, q_gain, tab, rope_mask):
    cqn = _rms(cq, qlg).astype(_BF16)
    for h in range(N_HEADS):
        q = _dot(cqn, wuq_ref[:, h * HEAD_PAD:(h + 1) * HEAD_PAD])
        a = q[:, :LANES]
        b = q[:, LANES:]
        ss = (jnp.sum(a * a, axis=-1, keepdims=True)
              + jnp.sum(jnp.where(rope_mask, b * b, 0.0), axis=-1, keepdims=True))
        r = lax.rsqrt(ss / QK_HEAD + EPS)
        yield a * r * q_gain[:, :LANES], b * r * q_gain[:, LANES:] * tab


def _kpe_rot(kpe_blk, k_gain, tab):
    kt = kpe_blk * k_gain * tab
    return kt + pltpu.roll(kt, QK_ROPE, axis=1)


def _head_select(cols, rows):
    lane = lax.broadcasted_iota(jnp.int32, (rows, N_HEADS), 1)
    out = jnp.zeros((rows, N_HEADS), _F32)
    for h, c in enumerate(cols):
        out = jnp.where(lane == h, c, out)
    return out


def _mla_prompt_kernel(x_ref, tab_ref, mk_ref, mv_ref, ng_ref, win_ref, qlg_ref, wuq_ref, qgain_ref,
                       ckvg_ref, wuk_ref, wuv_ref, kgain_ref, wout_ref, qg_ref,
                       y_ref, ckv_ref, kpe_ref, ksc_ref,
                       late, qbuf, kbuf, vbuf, obuf, o3, *, tm, tq, q_lora, kv_lora):
    i = pl.program_id(1)
    x = x_ref[0]
    tab = tab_ref[...]
    rope_mask = _rope_lane_mask(tm)
    xn = _rms(x, ng_ref[...]).astype(_BF16)
    o = q_lora + kv_lora
    early = _dot(xn, win_ref[:, :o + LANES])
    late[...] = _dot(xn, win_ref[:, o + LANES:])

    heads = _q_heads(early[:, :q_lora], qlg_ref[...], wuq_ref, qgain_ref[...] * (SM_SCALE * LOG2E), tab, rope_mask)
    for h, (a, b) in enumerate(heads):
        qbuf[h, :, :LANES] = a.astype(_BF16)
        qbuf[h, :, LANES:] = b.astype(_BF16)

    ckv = _rms(early[:, q_lora:o], ckvg_ref[...])
    ckv_ref[0] = ckv
    ckv_b = ckv.astype(_BF16)
    kpe_blk = early[:, o:o + LANES]
    skpe = jnp.sum(jnp.where(rope_mask, kpe_blk * kpe_blk, 0.0), axis=-1, keepdims=True)
    rot = _kpe_rot(kpe_blk, kgain_ref[...], tab)
    kpe_ref[0] = rot[:, :QK_ROPE]
    row0 = pl.multiple_of(i * tm, tm)
    kscs = []
    for h2 in range(0, N_HEADS, 2):
        k_pair = _dot(ckv_b, wuk_ref[:, h2 * QK_NOPE:(h2 + 2) * QK_NOPE])
        v_pair = _dot(ckv_b, wuv_ref[:, h2 * V_HEAD:(h2 + 2) * V_HEAD])
        for h in (h2, h2 + 1):
            kn = k_pair[:, (h - h2) * QK_NOPE:(h - h2 + 1) * QK_NOPE]
            ksc = lax.rsqrt((jnp.sum(kn * kn, axis=-1, keepdims=True) + skpe) / QK_HEAD + EPS)
            kscs.append(ksc)
            kbuf[h, pl.ds(row0, tm), :LANES] = (kn * ksc).astype(_BF16)
            kbuf[h, pl.ds(row0, tm), LANES:] = (rot * ksc).astype(_BF16)
            vbuf[h, pl.ds(row0, tm), :V_HEAD] = v_pair[:, (h - h2) * V_HEAD:(h - h2 + 1) * V_HEAD].astype(_BF16)
            vbuf[h, pl.ds(row0, tm), V_HEAD:] = jnp.ones((tm, LANES), _BF16)
    ksc_ref[0] = _head_select(kscs, tm)

    causal = (lax.broadcasted_iota(jnp.int32, (tq, tq), 0) >= lax.broadcasted_iota(jnp.int32, (tq, tq), 1))

    def attend_block(h, r0, n_keys):
        q = qbuf[h, r0:r0 + tq, :]
        s = _dot_nt(q, kbuf[h, 0:n_keys, :])
        diag = jnp.where(causal, s[:, n_keys - tq:], -jnp.inf)
        m = jnp.max(diag, axis=-1, keepdims=True)
        if n_keys > tq:
            past = s[:, :n_keys - tq]
            m = jnp.maximum(m, jnp.max(past, axis=-1, keepdims=True))
            p = jnp.concatenate([jnp.exp2(past - m), jnp.exp2(diag - m)], axis=1)
        else:
            p = jnp.exp2(diag - m)
        ov = _dot(p.astype(_BF16), vbuf[h, 0:n_keys, :])
        o3[h, r0:r0 + tq, :] = (ov[:, :V_HEAD] / ov[:, V_HEAD:]).astype(o3.dtype)

    for ti in range(kbuf.shape[1] // tm):
        @pl.when(i == ti)
        def _(ti=ti):
            def head_body(hp, carry):
                for r0 in range(0, tm, tq):
                    for dh in range(HEADS_PER_TRIP):
                        attend_block(HEADS_PER_TRIP * hp + dh, r0, ti * tm + r0 + tq)
                return carry

            lax.fori_loop(0, N_HEADS // HEADS_PER_TRIP, head_body, 0)

    d_mla = N_HEADS * V_HEAD
    zoff = D_MEM
    for h in range(N_HEADS):
        c0 = h * V_HEAD
        obuf[:, c0:c0 + V_HEAD] = (o3[h].astype(_F32) * _silu(late[:, zoff + c0:zoff + c0 + V_HEAD])).astype(_BF16)
    outs = _mem_attend(late[:, :D_MEM], mk_ref, mv_ref, 0, qg_ref[...])
    for h, om in enumerate(outs):
        c0 = d_mla + h * MEM_HEAD_DIM
        obuf[:, c0:c0 + MEM_HEAD_DIM] = (om * _silu(late[:, zoff + c0:zoff + c0 + MEM_HEAD_DIM])).astype(_BF16)
    y_ref[0] = x + _dot(obuf[...], wout_ref[...])


def _mla_prompt(x, tab, mem_k, mem_v, mem_layer, mem_q_g, norm_g, w_in, q_lora_g, w_uq, q_gain, ckv_g,
                w_uk, w_uv, k_gain, w_out, *, tm):
    b, t, d = x.shape
    q_lora = q_lora_g.shape[0]
    kv_lora = ckv_g.shape[0]
    mem_rows = mem_k.shape[1]
    off = mem_layer * b
    d_out = N_HEADS * V_HEAD + D_MEM
    tq = math.gcd(tm, ATTN_Q_ROWS)
    kern = functools.partial(_mla_prompt_kernel, tm=tm, tq=tq, q_lora=q_lora, kv_lora=kv_lora)
    tok = lambda w: pl.BlockSpec((1, tm, w), lambda i, j: (i, j, 0))
    return pl.pallas_call(
        kern,
        grid=(b, t // tm),
        in_specs=[
            tok(d),
            pl.BlockSpec((tm, LANES), lambda i, j: (j, 0)),
            pl.BlockSpec((1, mem_rows, MEM_HEAD_DIM), lambda i, j: (off + i, 0, 0), pipeline_mode=pl.Buffered(1)),
            pl.BlockSpec((1, mem_rows, MEM_HEAD_DIM), lambda i, j: (off + i, 0, 0), pipeline_mode=pl.Buffered(1)),
            _const_spec((1, d)),
            _const_spec(w_in.shape),
            _const_spec((1, q_lora)),
            _const_spec(w_uq.shape),
            _const_spec((1, HEAD_PAD)),
            _const_spec((1, kv_lora)),
            _const_spec(w_uk.shape),
            _const_spec(w_uv.shape),
            _const_spec((1, LANES)),
            _const_spec(w_out.shape),
            _const_spec((1, MEM_HEAD_DIM)),
        ],
        out_specs=[tok(d), tok(kv_lora), tok(QK_ROPE), tok(N_HEADS)],
        out_shape=[jax.ShapeDtypeStruct((b, t, d), _F32),
                   jax.ShapeDtypeStruct((b, t, kv_lora), _F32),
                   jax.ShapeDtypeStruct((b, t, QK_ROPE), _F32),
                   jax.ShapeDtypeStruct((b, t, N_HEADS), _F32)],
        scratch_shapes=[
            pltpu.VMEM((tm, D_MEM + d_out), _F32),
            pltpu.VMEM((N_HEADS, tm, HEAD_PAD), _BF16),
            pltpu.VMEM((N_HEADS, t, HEAD_PAD), _BF16),
            pltpu.VMEM((N_HEADS, t, V_HEAD + LANES), _BF16),
            pltpu.VMEM((tm, d_out), _BF16),
            pltpu.VMEM((N_HEADS, tm, V_HEAD), _BF16),
        ],
        compiler_params=_params(2),
        name="mla_prompt",
    )(x, tab, mem_k, mem_v, norm_g.reshape(1, d), w_in, q_lora_g.reshape(1, q_lora), w_uq, q_gain,
      ckv_g.reshape(1, kv_lora), w_uk, w_uv, k_gain, w_out, mem_q_g.reshape(1, MEM_HEAD_DIM))


def _mla_sample_proj_kernel(x_ref, tab_ref, ng_ref, win_ref, qlg_ref, wuq_ref, qgain_ref, ckvg_ref,
                            wuk_ref, wukt_ref, kgain_ref,
                            qabs_ref, qpe_ref, ckv_ref, kpe_ref, ksc_ref, rest_ref, *, q_lora, kv_lora):
    rows = x_ref.shape[0]
    tab = tab_ref[...]
    rope_mask = _rope_lane_mask(rows)
    xn = _rms(x_ref[...], ng_ref[...]).astype(_BF16)
    proj = _dot(xn, win_ref[...])
    o = q_lora + kv_lora
    rest_ref[...] = proj[:, o + LANES:]

    heads = _q_heads(proj[:, :q_lora], qlg_ref[...], wuq_ref, qgain_ref[...] * (SM_SCALE * LOG2E), tab, rope_mask)
    for h, (a, b) in enumerate(heads):
        qabs_ref[:, h * kv_lora:(h + 1) * kv_lora] = _dot(a.astype(_BF16), wukt_ref[h]).astype(_BF16)
        qpe_ref[:, h * LANES:(h + 1) * LANES] = (b + pltpu.roll(b, QK_ROPE, axis=1)).astype(_BF16)

    ckv = _rms(proj[:, q_lora:o], ckvg_ref[...])
    ckv_ref[...] = ckv
    k_nope = _dot(ckv.astype(_BF16), wuk_ref[...])
    kpe_blk = proj[:, o:o + LANES]
    skpe = jnp.sum(jnp.where(rope_mask, kpe_blk * kpe_blk, 0.0), axis=-1, keepdims=True)
    kpe_ref[...] = _kpe_rot(kpe_blk, kgain_ref[...], tab)[:, :QK_ROPE]
    kscs = []
    for h in range(N_HEADS):
        kn = k_nope[:, h * QK_NOPE:(h + 1) * QK_NOPE]
        kscs.append(lax.rsqrt((jnp.sum(kn * kn, axis=-1, keepdims=True) + skpe) / QK_HEAD + EPS))
    ksc_ref[...] = _head_select(kscs, rows)


def _mla_sample_proj(x2, tab, norm_g, w_in, q_lora_g, w_uq, q_gain, ckv_g, w_uk, w_ukt, k_gain, *, tm):
    n, d = x2.shape
    q_lora = q_lora_g.shape[0]
    kv_lora = ckv_g.shape[0]
    n_rest = w_in.shape[1] - (q_lora + kv_lora + LANES)
    kern = functools.partial(_mla_sample_proj_kernel, q_lora=q_lora, kv_lora=kv_lora)
    tok = lambda w: pl.BlockSpec((tm, w), lambda i: (i, 0))
    return pl.pallas_call(
        kern,
        grid=(n // tm,),
        in_specs=[
            tok(d), tok(LANES),
            _const_spec((1, d)),
            _const_spec(w_in.shape),
            _const_spec((1, q_lora)),
            _const_spec(w_uq.shape),
            _const_spec((1, HEAD_PAD)),
            _const_spec((1, kv_lora)),
            _const_spec(w_uk.shape),
            _const_spec(w_ukt.shape),
            _const_spec((1, LANES)),
        ],
        out_specs=[tok(N_HEADS * kv_lora), tok(N_HEADS * LANES), tok(kv_lora), tok(QK_ROPE), tok(N_HEADS),
                   tok(n_rest)],
        out_shape=[jax.ShapeDtypeStruct((n, N_HEADS * kv_lora), _BF16),
                   jax.ShapeDtypeStruct((n, N_HEADS * LANES), _BF16),
                   jax.ShapeDtypeStruct((n, kv_lora), _F32),
                   jax.ShapeDtypeStruct((n, QK_ROPE), _F32),
                   jax.ShapeDtypeStruct((n, N_HEADS), _F32),
                   jax.ShapeDtypeStruct((n, n_rest), _F32)],
        compiler_params=_params(1),
        name="mla_sample_proj",
    )(x2, tab, norm_g.reshape(1, d), w_in, q_lora_g.reshape(1, q_lora), w_uq, q_gain,
      ckv_g.reshape(1, kv_lora), w_uk, w_ukt, k_gain)


def _mla_sample_attn_kernel(pt_ref, qabs_ref, qpe_ref, nckv_ref, nkpe_ref, nksc_ref, ckv_hbm, kpe_hbm, ksc_hbm,
                            o_ref, ckv_buf, kpe_buf, ksc_buf, sems, ck_all, s_all, *, pps, n_blocks, n_slots, dec):
    b = pl.program_id(0)
    rows = N_HEADS * dec
    blk_keys = pps * PAGE_SIZE
    n_keys = n_blocks * blk_keys
    ahead = n_slots - 1

    def page_copies(bb, blk, slot):
        copies = []
        for p in range(pps):
            page = pt_ref[bb, blk * pps + p]
            keys = pl.ds(p * PAGE_SIZE, PAGE_SIZE)
            copies.append(pltpu.make_async_copy(ckv_hbm.at[page], ckv_buf.at[slot, keys, :], sems.at[0, slot]))
            copies.append(pltpu.make_async_copy(kpe_hbm.at[page], kpe_buf.at[slot, p], sems.at[1, slot]))
            copies.append(pltpu.make_async_copy(ksc_hbm.at[page], ksc_buf.at[slot, p], sems.at[2, slot]))
        return copies

    def slot_of(rel):
        return rel % n_slots if n_blocks % n_slots == 0 else lax.rem(b * n_blocks + rel, n_slots)

    def start_block(rel):
        db, blk = divmod(rel, n_blocks)

        def go():
            for c in page_copies(b + db, blk, slot_of(rel)):
                c.start()

        if db == 0:
            go()
        else:
            pl.when(b + db < pl.num_programs(0))(go)

    @pl.when(b == 0)
    def _():
        for rel in range(ahead):
            start_block(rel)

    qa = qabs_ref[0]
    qp = qpe_ref[0][:, :QK_ROPE]

    mx = jnp.full((rows, LANES), -jnp.inf, _F32)
    for blk in range(n_blocks):
        start_block(blk + ahead)
        slot = slot_of(blk)
        for c in page_copies(b, blk, slot):
            c.wait()
        ck = ckv_buf[slot].astype(_BF16)
        ck_all[blk * blk_keys:(blk + 1) * blk_keys, :] = ck
        kpt = jnp.concatenate([kpe_buf[slot, p] for p in range(pps)], axis=1).astype(_BF16)
        kst = jnp.concatenate([ksc_buf[slot, p] for p in range(pps)], axis=1)
        s = _dot_nt(qa, ck) + _dot(qp, kpt)
        s = (s.reshape(dec, N_HEADS, blk_keys) * kst[None]).reshape(rows, blk_keys)
        s_all[:, blk * blk_keys:(blk + 1) * blk_keys] = s
        for t in range(blk_keys // LANES):
            mx = jnp.maximum(mx, s[:, t * LANES:(t + 1) * LANES])

    nckv = nckv_ref[0].astype(_BF16)
    rr = lax.broadcasted_iota(jnp.int32, (rows, N_HEADS), 0)
    hh = lax.broadcasted_iota(jnp.int32, (rows, N_HEADS), 1)
    expand = (rr % N_HEADS == hh).astype(_F32)
    scale = lax.dot_general(expand, nksc_ref[0], (((1,), (1,)), ((), ())), precision=lax.Precision.HIGHEST,
                            preferred_element_type=_F32)
    s_new = (_dot_nt(qa, nckv) + _dot_nt(qp, nkpe_ref[0].astype(_BF16))) * scale
    tq = lax.broadcasted_iota(jnp.int32, (rows, dec), 0) // N_HEADS
    tk = lax.broadcasted_iota(jnp.int32, (rows, dec), 1)
    s_new = jnp.where(tk <= tq, s_new, -jnp.inf)

    m = jnp.maximum(jnp.max(mx, axis=-1, keepdims=True), jnp.max(s_new, axis=-1, keepdims=True))
    p_new = jnp.exp2(s_new - m)
    p = jnp.exp2(s_all[...] - m)
    l = jnp.sum(p, axis=-1, keepdims=True) + jnp.sum(p_new, axis=-1, keepdims=True)
    acc = _dot(p.astype(_BF16), ck_all[...]) + _dot(p_new.astype(_BF16), nckv)
    o_ref[0] = acc / l


def _mla_sample_attn(page_table, qabs, qpe, nckv, nkpe, nksc, cache_ckv, cache_kpe_t, cache_ksc_t):
    b, rows, kv_lora = qabs.shape
    dec = nckv.shape[1]
    n_pages = page_table.shape[1]
    pps = math.gcd(n_pages, PAGES_PER_STEP)
    n_blocks = n_pages // pps
    blk_keys = pps * PAGE_SIZE
    n_keys = n_pages * PAGE_SIZE
    n_slots = SAMPLE_DMA_SLOTS
    kern = functools.partial(_mla_sample_attn_kernel, pps=pps, n_blocks=n_blocks, n_slots=n_slots, dec=dec)
    per_b = lambda shape: pl.BlockSpec((1,) + shape, lambda i, pt: (i, 0, 0))
    hbm = pl.BlockSpec(memory_space=pl.ANY)
    grid_spec = pltpu.PrefetchScalarGridSpec(
        num_scalar_prefetch=1,
        grid=(b,),
        in_specs=[per_b((rows, kv_lora)), per_b((rows, LANES)), per_b((dec, kv_lora)), per_b((dec, QK_ROPE)),
                  per_b((dec, N_HEADS)), hbm, hbm, hbm],
        out_specs=per_b((rows, kv_lora)),
        scratch_shapes=[
            pltpu.VMEM((n_slots, blk_keys, kv_lora), _F32),
            pltpu.VMEM((n_slots, pps, QK_ROPE, PAGE_SIZE), _F32),
            pltpu.VMEM((n_slots, pps, N_HEADS, PAGE_SIZE), _F32),
            pltpu.SemaphoreType.DMA((3, n_slots)),
            pltpu.VMEM((n_keys, kv_lora), _BF16),
            pltpu.VMEM((rows, n_keys), _F32),
        ],
    )
    return pl.pallas_call(
        kern,
        grid_spec=grid_spec,
        out_shape=jax.ShapeDtypeStruct((b, rows, kv_lora), _F32),
        compiler_params=_params(1),
        name="mla_sample_attn",
    )(page_table, qabs, qpe, nckv, nkpe, nksc, cache_ckv, cache_kpe_t, cache_ksc_t)


def _mla_sample_out_kernel(x_ref, olat_ref, rest_ref, mk_ref, mv_ref, wuv_ref, wout_ref, qg_ref,
                           y_ref, obuf, *, nb, dec, kv_lora):
    d_mla = N_HEADS * V_HEAD
    zoff = D_MEM
    for h in range(N_HEADS):
        c0 = h * V_HEAD
        y_h = _dot(olat_ref[:, h * kv_lora:(h + 1) * kv_lora].astype(_BF16), wuv_ref[h])
        obuf[:, c0:c0 + V_HEAD] = (y_h * _silu(rest_ref[:, zoff + c0:zoff + c0 + V_HEAD])).astype(_BF16)
    qg = qg_ref[...]
    for bi in range(nb):
        r = slice(bi * dec, (bi + 1) * dec)
        outs = _mem_attend(rest_ref[r, :D_MEM], mk_ref, mv_ref, bi, qg)
        for h, om in enumerate(outs):
            c0 = d_mla + h * MEM_HEAD_DIM
            obuf[r, c0:c0 + MEM_HEAD_DIM] = (om * _silu(rest_ref[r, zoff + c0:zoff + c0 + MEM_HEAD_DIM])).astype(_BF16)
    y_ref[...] = x_ref[...] + _dot(obuf[...], wout_ref[...])


def _mla_sample_out(x2, olat, rest, mem_k, mem_v, mem_layer, mem_q_g, w_uv3, w_out, *, nb, dec):
    n, d = x2.shape
    kv_lora = w_uv3.shape[1]
    mem_rows = mem_k.shape[1]
    rows = nb * dec
    off = mem_layer * (n // rows)
    kern = functools.partial(_mla_sample_out_kernel, nb=nb, dec=dec, kv_lora=kv_lora)
    tok = lambda w: pl.BlockSpec((rows, w), lambda i: (i, 0))
    return pl.pallas_call(
        kern,
        grid=(n // rows,),
        in_specs=[
            tok(d), tok(olat.shape[1]), tok(rest.shape[1]),
            pl.BlockSpec((nb, mem_rows, MEM_HEAD_DIM), lambda i: (off + i, 0, 0)),
            pl.BlockSpec((nb, mem_rows, MEM_HEAD_DIM), lambda i: (off + i, 0, 0)),
            _const_spec(w_uv3.shape),
            _const_spec(w_out.shape),
            _const_spec((1, MEM_HEAD_DIM)),
        ],
        out_specs=tok(d),
        out_shape=jax.ShapeDtypeStruct((n, d), _F32),
        scratch_shapes=[pltpu.VMEM((rows, w_out.shape[0]), _BF16)],
        compiler_params=_params(1),
        name="mla_sample_out",
    )(x2, olat, rest, mem_k, mem_v, w_uv3, w_out, mem_q_g.reshape(1, MEM_HEAD_DIM))


def _pick_tile(n, target):
    t = min(n, target)
    while n % t or t % SUBLANES:
        t -= 1
    return t


def kernel(x_prompt, x_sample, mem_prompt, state_conv, cache_ckv, cache_kpe, cache_kscale, cache_mem_k, cache_mem_v, page_table, conv_norm_g, conv_w_in, conv_w, conv_w_out, mla_norm_g, mla_w_in, mla_q_lora_g, mla_w_uq, mla_ckv_g, mla_w_uk, mla_w_uv, mla_q_g, mla_k_g, mla_w_out, mem_norm_g, mem_w_kv, mem_q_g, mem_k_g):
    bp, seq, d = x_prompt.shape
    bs, dec, _ = x_sample.shape
    depth = mem_norm_g.shape[0]
    n_mem = mem_prompt.shape[1]
    dc = conv_w.shape[1]
    q_lora = mla_q_lora_g.shape[0]
    kv_lora = mla_ckv_g.shape[0]
    past = page_table.shape[1] * PAGE_SIZE

    tm_p = _pick_tile(seq, 512)
    nb_s = _pick_tile(bs, SAMPLE_BATCHES_PER_STEP) if bs % SUBLANES == 0 else bs

    mem_k, mem_v = _mem_kv(mem_prompt, mem_norm_g, mem_w_kv, mem_k_g)
    mem_rows = n_mem * MEM_HEADS
    mem_k2 = mem_k.reshape(depth * bp, mem_rows, MEM_HEAD_DIM)
    mem_v2 = mem_v.reshape(depth * bp, mem_rows, MEM_HEAD_DIM)
    smem_k2 = cache_mem_k.reshape(depth * bs, mem_rows, MEM_HEAD_DIM)
    smem_v2 = cache_mem_v.reshape(depth * bs, mem_rows, MEM_HEAD_DIM)

    cw_in = conv_w_in.astype(_BF16)
    cw_out = conv_w_out.astype(_BF16)
    hist_p = jnp.zeros((bp, CONV_WIDTH - 1, dc), _F32)
    y_p, conv_p = _conv_layer(x_prompt, hist_p, mem_k2, mem_v2, 0, mem_q_g[0], conv_norm_g, cw_in, conv_w, cw_out,
                              nb=1, tm=tm_p)
    y_s, conv_s = _conv_layer(x_sample, state_conv, smem_k2, smem_v2, 0, mem_q_g[0], conv_norm_g, cw_in, conv_w,
                              cw_out, nb=nb_s, tm=dec)

    w_in, w_uq, q_gain, k_gain = _mla_weights(mla_w_in, mla_w_uq, mla_q_g, mla_k_g, q_lora, kv_lora)
    w_uk2 = mla_w_uk.reshape(kv_lora, N_HEADS * QK_NOPE).astype(_BF16)
    w_uv2 = mla_w_uv.reshape(kv_lora, N_HEADS * V_HEAD).astype(_BF16)
    w_out = mla_w_out.astype(_BF16)
    tab_p = _rope_table(jnp.arange(seq, dtype=jnp.int32))
    y_p, ckv_p, kpe_p, ksc_p = _mla_prompt(y_p, tab_p, mem_k2, mem_v2, 1, mem_q_g[1], mla_norm_g, w_in,
                                           mla_q_lora_g, w_uq, q_gain, mla_ckv_g, w_uk2, w_uv2, k_gain, w_out,
                                           tm=tm_p)

    n_s = bs * dec
    tab_s = jnp.tile(_rope_table(past + jnp.arange(dec, dtype=jnp.int32)), (bs, 1))
    w_ukt = jnp.transpose(mla_w_uk, (1, 2, 0)).astype(_BF16)
    w_uv3 = jnp.transpose(mla_w_uv, (1, 0, 2)).astype(_BF16)
    qabs, qpe, ckv_s, kpe_s, ksc_s, rest = _mla_sample_proj(
        y_s.reshape(n_s, d), tab_s, mla_norm_g, w_in, mla_q_lora_g, w_uq, q_gain, mla_ckv_g, w_uk2, w_ukt, k_gain,
        tm=_pick_tile(n_s, 512))
    rows = dec * N_HEADS
    olat = _mla_sample_attn(page_table, qabs.reshape(bs, rows, kv_lora), qpe.reshape(bs, rows, LANES),
                            ckv_s.reshape(bs, dec, kv_lora), kpe_s.reshape(bs, dec, QK_ROPE),
                            ksc_s.reshape(bs, dec, N_HEADS), cache_ckv, jnp.transpose(cache_kpe, (0, 2, 1)),
                            jnp.transpose(cache_kscale, (0, 2, 1)))
    y_s2 = _mla_sample_out(y_s.reshape(n_s, d), olat.reshape(n_s, N_HEADS * kv_lora), rest, smem_k2, smem_v2, 1,
                           mem_q_g[1], w_uv3, w_out, nb=nb_s, dec=dec)

    new_shape = (depth, bp, n_mem, MEM_HEADS, MEM_HEAD_DIM)
    return (y_p, y_s2.reshape(bs, dec, d), conv_p, conv_s, ckv_p, kpe_p, ksc_p,
            ckv_s.reshape(bs, dec, kv_lora), kpe_s.reshape(bs, dec, QK_ROPE), ksc_s.reshape(bs, dec, N_HEADS),
            mem_k.reshape(new_shape), mem_v.reshape(new_shape))
```

```python
import functools
import math

import jax
import jax.numpy as jnp
from jax import lax
from jax.experimental import pallas as pl
from jax.experimental.pallas import tpu as pltpu

EPS = 1e-6
ROPE_THETA = 10000.0
PAGE_SIZE = 128
N_HEADS = 8
QK_NOPE = 128
QK_ROPE = 64
QK_HEAD = QK_NOPE + QK_ROPE
V_HEAD = 128
MEM_HEADS = 4
MEM_HEAD_DIM = 128
D_MEM = MEM_HEADS * MEM_HEAD_DIM
CONV_WIDTH = 3
SM_SCALE = QK_HEAD ** -0.5
MEM_SCALE = MEM_HEAD_DIM ** -0.5

LANES = 128
SUBLANES = 8
HEAD_PAD = 2 * LANES
VMEM_LIMIT = 60000 * 1024
PAGES_PER_STEP = 32
SAMPLE_DMA_SLOTS = 4
SAMPLE_BATCHES_PER_STEP = 16
ATTN_Q_ROWS = 256
LOG2E = math.log2(math.e)

_F32 = jnp.float32
_BF16 = jnp.bfloat16


def _dot(a, b):
    return jnp.dot(a, b, preferred_element_type=_F32)


def _dot_nt(a, b):
    return lax.dot_general(a, b, (((1,), (1,)), ((), ())), preferred_element_type=_F32)


def _rms(x, g):
    return x * lax.rsqrt(jnp.mean(x * x, axis=-1, keepdims=True) + EPS) * g


def _silu(z):
    h = 0.5 * z
    return h + h * jnp.tanh(h)


def _const_spec(shape):
    nd = len(shape)
    return pl.BlockSpec(shape, lambda *_: (0,) * nd, pipeline_mode=pl.Buffered(1))


def _params(n_axes):
    return pltpu.CompilerParams(dimension_semantics=("arbitrary",) * n_axes,
                                vmem_limit_bytes=VMEM_LIMIT)


def _mem_attend(qm, k_ref, v_ref, bi, qg):
    rows = qm.shape[0]
    mem_rows = k_ref.shape[1]
    n_mem = mem_rows // MEM_HEADS
    heads = [slice(h * MEM_HEAD_DIM, (h + 1) * MEM_HEAD_DIM) for h in range(MEM_HEADS)]
    if rows * MEM_HEADS <= LANES:
        q = jnp.concatenate([_rms(qm[:, sl], qg) * (MEM_SCALE * LOG2E) for sl in heads], axis=0).astype(_BF16)
        s = _dot_nt(q, k_ref[bi].astype(_BF16))
        q_head = lax.broadcasted_iota(jnp.int32, s.shape, 0) // rows
        k_head = lax.broadcasted_iota(jnp.int32, s.shape, 1) % MEM_HEADS
        s = jnp.where(q_head == k_head, s, -jnp.inf)
        p = jnp.exp2(s - jnp.max(s, axis=-1, keepdims=True))
        o = _dot(p.astype(_BF16), v_ref[bi].astype(_BF16)) / jnp.sum(p, axis=-1, keepdims=True)
        return [o[h * rows:(h + 1) * rows] for h in range(MEM_HEADS)]
    outs = []
    for sl in heads:
        h = sl.start // MEM_HEAD_DIM
        q = (_rms(qm[:, sl], qg) * (MEM_SCALE * LOG2E)).astype(_BF16)
        k = k_ref[bi, pl.ds(h, n_mem, stride=MEM_HEADS), :].astype(_BF16)
        v = v_ref[bi, pl.ds(h, n_mem, stride=MEM_HEADS), :].astype(_BF16)
        s = _dot_nt(q, k)
        p = jnp.exp2(s - jnp.max(s, axis=-1, keepdims=True))
        o = _dot(p.astype(_BF16), v) / jnp.sum(p, axis=-1, keepdims=True)
        outs.append(o)
    return outs


def _mem_kv_kernel(mem_ref, ng_ref, w_ref, kg_ref, k_out, v_out):
    x = mem_ref[0]
    xn = _rms(x, ng_ref[0]).astype(_BF16)
    kv = _dot(xn, w_ref[0])
    kg = kg_ref[0]
    m = x.shape[0]
    for h in range(MEM_HEADS):
        sl = slice(h * MEM_HEAD_DIM, (h + 1) * MEM_HEAD_DIM)
        rows = pl.ds(h, m, stride=MEM_HEADS)
        k_out[0, 0, rows, :] = _rms(kv[:, sl], kg)
        v_out[0, 0, rows, :] = kv[:, D_MEM + h * MEM_HEAD_DIM:D_MEM + (h + 1) * MEM_HEAD_DIM]


def _mem_kv(mem, norm_g, w_kv, k_g):
    b, m, d = mem.shape
    depth = norm_g.shape[0]
    out = jax.ShapeDtypeStruct((depth, b, m * MEM_HEADS, MEM_HEAD_DIM), _F32)
    return pl.pallas_call(
        _mem_kv_kernel,
        grid=(depth, b),
        in_specs=[
            pl.BlockSpec((1, m, d), lambda l, i: (i, 0, 0)),
            pl.BlockSpec((1, 1, d), lambda l, i: (l, 0, 0)),
            pl.BlockSpec((1, d, 2 * D_MEM), lambda l, i: (l, 0, 0)),
            pl.BlockSpec((1, 1, MEM_HEAD_DIM), lambda l, i: (l, 0, 0)),
        ],
        out_specs=[pl.BlockSpec((1, 1, m * MEM_HEADS, MEM_HEAD_DIM), lambda l, i: (l, i, 0, 0))] * 2,
        out_shape=[out, out],
        compiler_params=_params(2),
        name="mem_kv",
    )(mem, norm_g.reshape(depth, 1, d), w_kv.astype(_BF16), k_g.reshape(depth, 1, MEM_HEAD_DIM))


def _conv_layer_kernel(x_ref, hist_ref, mk_ref, mv_ref, ng_ref, win_ref, cw_ref, wout_ref, qg_ref,
                       y_ref, state_ref, cbuf, obuf, *, nb, tm, dc):
    rows = nb * tm
    d = x_ref.shape[-1]

    @pl.when(pl.program_id(1) == 0)
    def _():
        cbuf[:, SUBLANES - 2:SUBLANES, :] = hist_ref[...]

    x = x_ref[...].reshape(rows, d)
    xn = _rms(x, ng_ref[...]).astype(_BF16)
    proj = _dot(xn, win_ref[...])
    cu = proj[:, 2 * dc:3 * dc] * proj[:, :dc]
    cbuf[:, SUBLANES:SUBLANES + tm, :] = cu.reshape(nb, tm, dc)
    cw = cw_ref[...]
    conv = (cw[0:1] * cbuf[:, SUBLANES - 2:SUBLANES - 2 + tm, :].reshape(rows, dc)
            + cw[1:2] * cbuf[:, SUBLANES - 1:SUBLANES - 1 + tm, :].reshape(rows, dc)
            + cw[2:3] * cu)
    state = cbuf[:, tm + SUBLANES - 2:tm + SUBLANES, :]
    state_ref[...] = state
    cbuf[:, SUBLANES - 2:SUBLANES, :] = state
    zoff = 3 * dc + D_MEM
    obuf[:, :dc] = (proj[:, dc:2 * dc] * conv * _silu(proj[:, zoff:zoff + dc])).astype(_BF16)
    qg = qg_ref[...]
    for bi in range(nb):
        r = slice(bi * tm, (bi + 1) * tm)
        outs = _mem_attend(proj[r, 3 * dc:3 * dc + D_MEM], mk_ref, mv_ref, bi, qg)
        for h, o in enumerate(outs):
            c0 = dc + h * MEM_HEAD_DIM
            obuf[r, c0:c0 + MEM_HEAD_DIM] = (o * _silu(proj[r, zoff + c0:zoff + c0 + MEM_HEAD_DIM])).astype(_BF16)
    y_ref[...] = (x + _dot(obuf[...], wout_ref[...])).reshape(nb, tm, d)


def _conv_layer(x, hist, mem_k, mem_v, mem_layer, mem_q_g, norm_g, w_in, conv_w, w_out, *, nb, tm):
    b, t, d = x.shape
    dc = conv_w.shape[1]
    mem_rows = mem_k.shape[1]
    off = mem_layer * (b // nb)
    kern = functools.partial(_conv_layer_kernel, nb=nb, tm=tm, dc=dc)
    return pl.pallas_call(
        kern,
        grid=(b // nb, t // tm),
        in_specs=[
            pl.BlockSpec((nb, tm, d), lambda i, j: (i, j, 0)),
            pl.BlockSpec((nb, CONV_WIDTH - 1, dc), lambda i, j: (i, 0, 0)),
            pl.BlockSpec((nb, mem_rows, MEM_HEAD_DIM), lambda i, j: (off + i, 0, 0)),
            pl.BlockSpec((nb, mem_rows, MEM_HEAD_DIM), lambda i, j: (off + i, 0, 0)),
            _const_spec((1, d)),
            _const_spec(w_in.shape),
            _const_spec(conv_w.shape),
            _const_spec(w_out.shape),
            _const_spec((1, MEM_HEAD_DIM)),
        ],
        out_specs=[
            pl.BlockSpec((nb, tm, d), lambda i, j: (i, j, 0)),
            pl.BlockSpec((nb, CONV_WIDTH - 1, dc), lambda i, j: (i, 0, 0)),
        ],
        out_shape=[jax.ShapeDtypeStruct((b, t, d), _F32),
                   jax.ShapeDtypeStruct((b, CONV_WIDTH - 1, dc), _F32)],
        scratch_shapes=[pltpu.VMEM((nb, tm + SUBLANES, dc), _F32),
                        pltpu.VMEM((nb * tm, dc + D_MEM), _BF16)],
        compiler_params=_params(2),
        name="conv_layer",
    )(x, hist, mem_k, mem_v, norm_g.reshape(1, d), w_in, conv_w, w_out, mem_q_g.reshape(1, MEM_HEAD_DIM))


def _rope_table(pos):
    inv = 1.0 / (ROPE_THETA ** (jnp.arange(0, QK_ROPE, 2, dtype=_F32) / QK_ROPE))
    ang = pos.astype(_F32)[:, None] * inv[None, :]
    c, s = jnp.cos(ang), jnp.sin(ang)
    return jnp.concatenate([c, c, -s, s], axis=-1)


def _swap_halves(w):
    half = w.shape[-1] // 2
    return jnp.concatenate([w[..., half:], w[..., :half]], axis=-1)


def _mla_weights(mla_w_in, mla_w_uq, mla_q_g, mla_k_g, q_lora, kv_lora):
    o = q_lora + kv_lora
    kpe_w = mla_w_in[:, o:o + QK_ROPE]
    w_in = jnp.concatenate([mla_w_in[:, :o], kpe_w, _swap_halves(kpe_w), mla_w_in[:, o + QK_ROPE:]], axis=1)
    wq = mla_w_uq.reshape(q_lora, N_HEADS, QK_HEAD)
    wq = jnp.concatenate([wq, _swap_halves(wq[..., QK_NOPE:])], axis=-1)
    w_uq = wq.reshape(q_lora, N_HEADS * HEAD_PAD)
    g_rope = mla_q_g[QK_NOPE:]
    q_gain = jnp.concatenate([mla_q_g[:QK_NOPE] * mla_k_g[:QK_NOPE], g_rope, _swap_halves(g_rope)])
    k_rope = mla_k_g[QK_NOPE:]
    k_gain = jnp.concatenate([k_rope, _swap_halves(k_rope)])
    return (w_in.astype(_BF16), w_uq.astype(_BF16), q_gain.reshape(1, HEAD_PAD), k_gain.reshape(1, LANES))


def _rope_lane_mask(rows):
    return lax.broadcasted_iota(jnp.int32, (rows, LANES), 1) < QK_ROPE


def _q_heads(cq, qlg, wuq_ref, q_gain, tab, rope_mask):
    cqn = _rms(cq, qlg).astype(_BF16)
    for h in range(N_HEADS):
        q = _dot(cqn, wuq_ref[:, h * HEAD_PAD:(h + 1) * HEAD_PAD])
        a = q[:, :LANES]
        b = q[:, LANES:]
        ss = jnp.sum(a * a + jnp.where(rope_mask, b * b, 0.0), axis=-1, keepdims=True)
        r = lax.rsqrt(ss / QK_HEAD + EPS)
        yield a * r * q_gain[:, :LANES], b * r * q_gain[:, LANES:] * tab


def _kpe_rot(kpe_blk, k_gain, tab):
    kt = kpe_blk * k_gain * tab
    return kt + pltpu.roll(kt, QK_ROPE, axis=1)


def _head_select(cols, rows):
    lane = lax.broadcasted_iota(jnp.int32, (rows, N_HEADS), 1)
    out = jnp.zeros((rows, N_HEADS), _F32)
    for h, c in enumerate(cols):
        out = jnp.where(lane == h, c, out)
    return out


def _mla_prompt_kernel(x_ref, tab_ref, mk_ref, mv_ref, ng_ref, win_ref, qlg_ref, wuq_ref, qgain_ref,
                       ckvg_ref, wuk_ref, wuv_ref, kgain_ref, wout_ref, qg_ref,
                       y_ref, ckv_ref, kpe_ref, ksc_ref,
                       late, qbuf, kbuf, vbuf, obuf, *, tm, tq, q_lora, kv_lora):
    i = pl.program_id(1)
    x = x_ref[0]
    tab = tab_ref[...]
    rope_mask = _rope_lane_mask(tm)
    xn = _rms(x, ng_ref[...]).astype(_BF16)
    o = q_lora + kv_lora
    early = _dot(xn, win_ref[:, :o + LANES])
    late[...] = _dot(xn, win_ref[:, o + LANES:])

    heads = _q_heads(early[:, :q_lora], qlg_ref[...], wuq_ref, qgain_ref[...] * (SM_SCALE * LOG2E), tab, rope_mask)
    for h, (a, b) in enumerate(heads):
        qbuf[h, :, :LANES] = a.astype(_BF16)
        qbuf[h, :, LANES:] = b.astype(_BF16)

    ckv = _rms(early[:, q_lora:o], ckvg_ref[...])
    ckv_ref[0] = ckv
    ckv_b = ckv.astype(_BF16)
    kpe_blk = early[:, o:o + LANES]
    skpe = jnp.sum(jnp.where(rope_mask, kpe_blk * kpe_blk, 0.0), axis=-1, keepdims=True)
    rot = _kpe_rot(kpe_blk, kgain_ref[...], tab)
    kpe_ref[0] = rot[:, :QK_ROPE]
    row0 = pl.multiple_of(i * tm, tm)
    kscs = []
    for h2 in range(0, N_HEADS, 2):
        k_pair = _dot(ckv_b, wuk_ref[:, h2 * QK_NOPE:(h2 + 2) * QK_NOPE])
        v_pair = _dot(ckv_b, wuv_ref[:, h2 * V_HEAD:(h2 + 2) * V_HEAD])
        for h in (h2, h2 + 1):
            kn = k_pair[:, (h - h2) * QK_NOPE:(h - h2 + 1) * QK_NOPE]
            ksc = lax.rsqrt((jnp.sum(kn * kn, axis=-1, keepdims=True) + skpe) / QK_HEAD + EPS)
            kscs.append(ksc)
            kbuf[h, pl.ds(row0, tm), :LANES] = (kn * ksc).astype(_BF16)
            kbuf[h, pl.ds(row0, tm), LANES:] = (rot * ksc).astype(_BF16)
            vbuf[h, pl.ds(row0, tm), :V_HEAD] = v_pair[:, (h - h2) * V_HEAD:(h - h2 + 1) * V_HEAD].astype(_BF16)
            vbuf[h, pl.ds(row0, tm), V_HEAD:] = jnp.ones((tm, LANES), _BF16)
    ksc_ref[0] = _head_select(kscs, tm)

    causal = (lax.broadcasted_iota(jnp.int32, (tq, tq), 0) >= lax.broadcasted_iota(jnp.int32, (tq, tq), 1))

    def attend_block(h, r0, n_keys):
        q = qbuf[h, r0:r0 + tq, :]
        s = _dot_nt(q, kbuf[h, 0:n_keys, :])
        diag = jnp.where(causal, s[:, n_keys - tq:], -jnp.inf)
        m = jnp.max(diag, axis=-1, keepdims=True)
        if n_keys > tq:
            past = s[:, :n_keys - tq]
            m = jnp.maximum(m, jnp.max(past, axis=-1, keepdims=True))
            p = jnp.concatenate([jnp.exp2(past - m), jnp.exp2(diag - m)], axis=1)
        else:
            p = jnp.exp2(diag - m)
        ov = _dot(p.astype(_BF16), vbuf[h, 0:n_keys, :])
        cols = slice(h * V_HEAD, (h + 1) * V_HEAD)
        gate = _silu(late[r0:r0 + tq, zoff + h * V_HEAD:zoff + (h + 1) * V_HEAD])
        obuf[r0:r0 + tq, cols] = (ov[:, :V_HEAD] / ov[:, V_HEAD:] * gate).astype(_BF16)

    d_mla = N_HEADS * V_HEAD
    zoff = D_MEM
    for ti in range(kbuf.shape[1] // tm):
        @pl.when(i == ti)
        def _(ti=ti):
            for r0 in range(0, tm, tq):
                for h in range(N_HEADS):
                    attend_block(h, r0, ti * tm + r0 + tq)

    outs = _mem_attend(late[:, :D_MEM], mk_ref, mv_ref, 0, qg_ref[...])
    for h, om in enumerate(outs):
        c0 = d_mla + h * MEM_HEAD_DIM
        obuf[:, c0:c0 + MEM_HEAD_DIM] = (om * _silu(late[:, zoff + c0:zoff + c0 + MEM_HEAD_DIM])).astype(_BF16)
    y_ref[0] = x + _dot(obuf[...], wout_ref[...])


def _mla_prompt(x, tab, mem_k, mem_v, mem_layer, mem_q_g, norm_g, w_in, q_lora_g, w_uq, q_gain, ckv_g,
                w_uk, w_uv, k_gain, w_out, *, tm):
    b, t, d = x.shape
    q_lora = q_lora_g.shape[0]
    kv_lora = ckv_g.shape[0]
    mem_rows = mem_k.shape[1]
    off = mem_layer * b
    d_out = N_HEADS * V_HEAD + D_MEM
    tq = math.gcd(tm, ATTN_Q_ROWS)
    kern = functools.partial(_mla_prompt_kernel, tm=tm, tq=tq, q_lora=q_lora, kv_lora=kv_lora)
    tok = lambda w: pl.BlockSpec((1, tm, w), lambda i, j: (i, j, 0))
    return pl.pallas_call(
        kern,
        grid=(b, t // tm),
        in_specs=[
            tok(d),
            pl.BlockSpec((tm, LANES), lambda i, j: (j, 0)),
            pl.BlockSpec((1, mem_rows, MEM_HEAD_DIM), lambda i, j: (off + i, 0, 0), pipeline_mode=pl.Buffered(1)),
            pl.BlockSpec((1, mem_rows, MEM_HEAD_DIM), lambda i, j: (off + i, 0, 0), pipeline_mode=pl.Buffered(1)),
            _const_spec((1, d)),
            _const_spec(w_in.shape),
            _const_spec((1, q_lora)),
            _const_spec(w_uq.shape),
            _const_spec((1, HEAD_PAD)),
            _const_spec((1, kv_lora)),
            _const_spec(w_uk.shape),
            _const_spec(w_uv.shape),
            _const_spec((1, LANES)),
            _const_spec(w_out.shape),
            _const_spec((1, MEM_HEAD_DIM)),
        ],
        out_specs=[tok(d), tok(kv_lora), tok(QK_ROPE), tok(N_HEADS)],
        out_shape=[jax.ShapeDtypeStruct((b, t, d), _F32),
                   jax.ShapeDtypeStruct((b, t, kv_lora), _F32),
                   jax.ShapeDtypeStruct((b, t, QK_ROPE), _F32),
                   jax.ShapeDtypeStruct((b, t, N_HEADS), _F32)],
        scratch_shapes=[
            pltpu.VMEM((tm, D_MEM + d_out), _F32),
            pltpu.VMEM((N_HEADS, tm, HEAD_PAD), _BF16),
            pltpu.VMEM((N_HEADS, t, HEAD_PAD), _BF16),
            pltpu.VMEM((N_HEADS, t, V_HEAD + LANES), _BF16),
            pltpu.VMEM((tm, d_out), _BF16),
        ],
        compiler_params=_params(2),
        name="mla_prompt",
    )(x, tab, mem_k, mem_v, norm_g.reshape(1, d), w_in, q_lora_g.reshape(1, q_lora), w_uq, q_gain,
      ckv_g.reshape(1, kv_lora), w_uk, w_uv, k_gain, w_out, mem_q_g.reshape(1, MEM_HEAD_DIM))


def _mla_sample_proj_kernel(x_ref, tab_ref, ng_ref, win_ref, qlg_ref, wuq_ref, qgain_ref, ckvg_ref,
                            wuk_ref, wukt_ref, kgain_ref,
                            qabs_ref, qpe_ref, ckv_ref, kpe_ref, ksc_ref, rest_ref, *, q_lora, kv_lora):
    rows = x_ref.shape[0]
    tab = tab_ref[...]
    rope_mask = _rope_lane_mask(rows)
    xn = _rms(x_ref[...], ng_ref[...]).astype(_BF16)
    proj = _dot(xn, win_ref[...])
    o = q_lora + kv_lora
    rest_ref[...] = proj[:, o + LANES:]

    heads = _q_heads(proj[:, :q_lora], qlg_ref[...], wuq_ref, qgain_ref[...] * (SM_SCALE * LOG2E), tab, rope_mask)
    for h, (a, b) in enumerate(heads):
        qabs_ref[:, h * kv_lora:(h + 1) * kv_lora] = _dot(a.astype(_BF16), wukt_ref[h]).astype(_BF16)
        qpe_ref[:, h * LANES:(h + 1) * LANES] = (b + pltpu.roll(b, QK_ROPE, axis=1)).astype(_BF16)

    ckv = _rms(proj[:, q_lora:o], ckvg_ref[...])
    ckv_ref[...] = ckv
    k_nope = _dot(ckv.astype(_BF16), wuk_ref[...])
    kpe_blk = proj[:, o:o + LANES]
    skpe = jnp.sum(jnp.where(rope_mask, kpe_blk * kpe_blk, 0.0), axis=-1, keepdims=True)
    kpe_ref[...] = _kpe_rot(kpe_blk, kgain_ref[...], tab)[:, :QK_ROPE]
    kscs = []
    for h in range(N_HEADS):
        kn = k_nope[:, h * QK_NOPE:(h + 1) * QK_NOPE]
        kscs.append(lax.rsqrt((jnp.sum(kn * kn, axis=-1, keepdims=True) + skpe) / QK_HEAD + EPS))
    ksc_ref[...] = _head_select(kscs, rows)


def _mla_sample_proj(x2, tab, norm_g, w_in, q_lora_g, w_uq, q_gain, ckv_g, w_uk, w_ukt, k_gain, *, tm):
    n, d = x2.shape
    q_lora = q_lora_g.shape[0]
    kv_lora = ckv_g.shape[0]
    n_rest = w_in.shape[1] - (q_lora + kv_lora + LANES)
    kern = functools.partial(_mla_sample_proj_kernel, q_lora=q_lora, kv_lora=kv_lora)
    tok = lambda w: pl.BlockSpec((tm, w), lambda i: (i, 0))
    return pl.pallas_call(
        kern,
        grid=(n // tm,),
        in_specs=[
            tok(d), tok(LANES),
            _const_spec((1, d)),
            _const_spec(w_in.shape),
            _const_spec((1, q_lora)),
            _const_spec(w_uq.shape),
            _const_spec((1, HEAD_PAD)),
            _const_spec((1, kv_lora)),
            _const_spec(w_uk.shape),
            _const_spec(w_ukt.shape),
            _const_spec((1, LANES)),
        ],
        out_specs=[tok(N_HEADS * kv_lora), tok(N_HEADS * LANES), tok(kv_lora), tok(QK_ROPE), tok(N_HEADS),
                   tok(n_rest)],
        out_shape=[jax.ShapeDtypeStruct((n, N_HEADS * kv_lora), _BF16),
                   jax.ShapeDtypeStruct((n, N_HEADS * LANES), _BF16),
                   jax.ShapeDtypeStruct((n, kv_lora), _F32),
                   jax.ShapeDtypeStruct((n, QK_ROPE), _F32),
                   jax.ShapeDtypeStruct((n, N_HEADS), _F32),
                   jax.ShapeDtypeStruct((n, n_rest), _F32)],
        compiler_params=_params(1),
        name="mla_sample_proj",
    )(x2, tab, norm_g.reshape(1, d), w_in, q_lora_g.reshape(1, q_lora), w_uq, q_gain,
      ckv_g.reshape(1, kv_lora), w_uk, w_ukt, k_gain)


def _mla_sample_attn_kernel(pt_ref, qabs_ref, qpe_ref, nckv_ref, nkpe_ref, nksc_ref, ckv_hbm, kpe_hbm, ksc_hbm,
                            o_ref, ckv_buf, kpe_buf, ksc_buf, sems, ck_all, s_all, *, pps, n_blocks, n_slots, dec):
    b = pl.program_id(0)
    rows = N_HEADS * dec
    blk_keys = pps * PAGE_SIZE
    n_keys = n_blocks * blk_keys
    ahead = n_slots - 1

    def page_copies(bb, blk, slot):
        copies = []
        for p in range(pps):
            page = pt_ref[bb, blk * pps + p]
            keys = pl.ds(p * PAGE_SIZE, PAGE_SIZE)
            copies.append(pltpu.make_async_copy(ckv_hbm.at[page], ckv_buf.at[slot, keys, :], sems.at[0, slot]))
            copies.append(pltpu.make_async_copy(kpe_hbm.at[page], kpe_buf.at[slot, p], sems.at[1, slot]))
            copies.append(pltpu.make_async_copy(ksc_hbm.at[page], ksc_buf.at[slot, p], sems.at[2, slot]))
        return copies

    def slot_of(rel):
        return rel % n_slots if n_blocks % n_slots == 0 else lax.rem(b * n_blocks + rel, n_slots)

    def start_block(rel):
        db, blk = divmod(rel, n_blocks)

        def go():
            for c in page_copies(b + db, blk, slot_of(rel)):
                c.start()

        if db == 0:
            go()
        else:
            pl.when(b + db < pl.num_programs(0))(go)

    @pl.when(b == 0)
    def _():
        for rel in range(ahead):
            start_block(rel)

    qa = qabs_ref[0]
    qp = qpe_ref[0][:, :QK_ROPE]

    mx = jnp.full((rows, LANES), -jnp.inf, _F32)
    for blk in range(n_blocks):
        start_block(blk + ahead)
        slot = slot_of(blk)
        for c in page_copies(b, blk, slot):
            c.wait()
        ck = ckv_buf[slot].astype(_BF16)
        ck_all[blk * blk_keys:(blk + 1) * blk_keys, :] = ck
        kpt = jnp.concatenate([kpe_buf[slot, p] for p in range(pps)], axis=1).astype(_BF16)
        kst = jnp.concatenate([ksc_buf[slot, p] for p in range(pps)], axis=1)
        s = _dot_nt(qa, ck) + _dot(qp, kpt)
        s = (s.reshape(dec, N_HEADS, blk_keys) * kst[None]).reshape(rows, blk_keys)
        s_all[:, blk * blk_keys:(blk + 1) * blk_keys] = s
        for t in range(blk_keys // LANES):
            mx = jnp.maximum(mx, s[:, t * LANES:(t + 1) * LANES])

    nckv = nckv_ref[0].astype(_BF16)
    rr = lax.broadcasted_iota(jnp.int32, (rows, N_HEADS), 0)
    hh = lax.broadcasted_iota(jnp.int32, (rows, N_HEADS), 1)
    expand = (rr % N_HEADS == hh).astype(_F32)
    scale = lax.dot_general(expand, nksc_ref[0], (((1,), (1,)), ((), ())), precision=lax.Precision.HIGHEST,
                            preferred_element_type=_F32)
    s_new = (_dot_nt(qa, nckv) + _dot_nt(qp, nkpe_ref[0].astype(_BF16))) * scale
    tq = lax.broadcasted_iota(jnp.int32, (rows, dec), 0) // N_HEADS
    tk = lax.broadcasted_iota(jnp.int32, (rows, dec), 1)
    s_new = jnp.where(tk <= tq, s_new, -jnp.inf)

    m = jnp.maximum(jnp.max(mx, axis=-1, keepdims=True), jnp.max(s_new, axis=-1, keepdims=True))
    p_new = jnp.exp2(s_new - m)
    p = jnp.exp2(s_all[...] - m)
    l = jnp.sum(p, axis=-1, keepdims=True) + jnp.sum(p_new, axis=-1, keepdims=True)
    acc = _dot(p.astype(_BF16), ck_all[...]) + _dot(p_new.astype(_BF16), nckv)
    o_ref[0] = acc / l


def _mla_sample_attn(page_table, qabs, qpe, nckv, nkpe, nksc, cache_ckv, cache_kpe_t, cache_ksc_t):
    b, rows, kv_lora = qabs.shape
    dec = nckv.shape[1]
    n_pages = page_table.shape[1]
    pps = math.gcd(n_pages, PAGES_PER_STEP)
    n_blocks = n_pages // pps
    blk_keys = pps * PAGE_SIZE
    n_keys = n_pages * PAGE_SIZE
    n_slots = SAMPLE_DMA_SLOTS
    kern = functools.partial(_mla_sample_attn_kernel, pps=pps, n_blocks=n_blocks, n_slots=n_slots, dec=dec)
    per_b = lambda shape: pl.BlockSpec((1,) + shape, lambda i, pt: (i, 0, 0))
    hbm = pl.BlockSpec(memory_space=pl.ANY)
    grid_spec = pltpu.PrefetchScalarGridSpec(
        num_scalar_prefetch=1,
        grid=(b,),
        in_specs=[per_b((rows, kv_lora)), per_b((rows, LANES)), per_b((dec, kv_lora)), per_b((dec, QK_ROPE)),
                  per_b((dec, N_HEADS)), hbm, hbm, hbm],
        out_specs=per_b((rows, kv_lora)),
        scratch_shapes=[
            pltpu.VMEM((n_slots, blk_keys, kv_lora), _F32),
            pltpu.VMEM((n_slots, pps, QK_ROPE, PAGE_SIZE), _F32),
            pltpu.VMEM((n_slots, pps, N_HEADS, PAGE_SIZE), _F32),
            pltpu.SemaphoreType.DMA((3, n_slots)),
            pltpu.VMEM((n_keys, kv_lora), _BF16),
            pltpu.VMEM((rows, n_keys), _F32),
        ],
    )
    return pl.pallas_call(
        kern,
        grid_spec=grid_spec,
        out_shape=jax.ShapeDtypeStruct((b, rows, kv_lora), _F32),
        compiler_params=_params(1),
        name="mla_sample_attn",
    )(page_table, qabs, qpe, nckv, nkpe, nksc, cache_ckv, cache_kpe_t, cache_ksc_t)


def _mla_sample_out_kernel(x_ref, olat_ref, rest_ref, mk_ref, mv_ref, wuv_ref, wout_ref, qg_ref,
                           y_ref, obuf, *, nb, dec, kv_lora):
    d_mla = N_HEADS * V_HEAD
    zoff = D_MEM
    for h in range(N_HEADS):
        c0 = h * V_HEAD
        y_h = _dot(olat_ref[:, h * kv_lora:(h + 1) * kv_lora].astype(_BF16), wuv_ref[h])
        obuf[:, c0:c0 + V_HEAD] = (y_h * _silu(rest_ref[:, zoff + c0:zoff + c0 + V_HEAD])).astype(_BF16)
    qg = qg_ref[...]
    for bi in range(nb):
        r = slice(bi * dec, (bi + 1) * dec)
        outs = _mem_attend(rest_ref[r, :D_MEM], mk_ref, mv_ref, bi, qg)
        for h, om in enumerate(outs):
            c0 = d_mla + h * MEM_HEAD_DIM
            obuf[r, c0:c0 + MEM_HEAD_DIM] = (om * _silu(rest_ref[r, zoff + c0:zoff + c0 + MEM_HEAD_DIM])).astype(_BF16)
    y_ref[...] = x_ref[...] + _dot(obuf[...], wout_ref[...])


def _mla_sample_out(x2, olat, rest, mem_k, mem_v, mem_layer, mem_q_g, w_uv3, w_out, *, nb, dec):
    n, d = x2.shape
    kv_lora = w_uv3.shape[1]
    mem_rows = mem_k.shape[1]
    rows = nb * dec
    off = mem_layer * (n // rows)
    kern = functools.partial(_mla_sample_out_kernel, nb=nb, dec=dec, kv_lora=kv_lora)
    tok = lambda w: pl.BlockSpec((rows, w), lambda i: (i, 0))
    return pl.pallas_call(
        kern,
        grid=(n // rows,),
        in_specs=[
            tok(d), tok(olat.shape[1]), tok(rest.shape[1]),
            pl.BlockSpec((nb, mem_rows, MEM_HEAD_DIM), lambda i: (off + i, 0, 0)),
            pl.BlockSpec((nb, mem_rows, MEM_HEAD_DIM), lambda i: (off + i, 0, 0)),
            _const_spec(w_uv3.shape),
            _const_spec(w_out.shape),
            _const_spec((1, MEM_HEAD_DIM)),
        ],
        out_specs=tok(d),
        out_shape=jax.ShapeDtypeStruct((n, d), _F32),
        scratch_shapes=[pltpu.VMEM((rows, w_out.shape[0]), _BF16)],
        compiler_params=_params(1),
        name="mla_sample_out",
    )(x2, olat, rest, mem_k, mem_v, w_uv3, w_out, mem_q_g.reshape(1, MEM_HEAD_DIM))


def _pick_tile(n, target):
    t = min(n, target)
    while n % t or t % SUBLANES:
        t -= 1
    return t


def kernel(x_prompt, x_sample, mem_prompt, state_conv, cache_ckv, cache_kpe, cache_kscale, cache_mem_k, cache_mem_v, page_table, conv_norm_g, conv_w_in, conv_w, conv_w_out, mla_norm_g, mla_w_in, mla_q_lora_g, mla_w_uq, mla_ckv_g, mla_w_uk, mla_w_uv, mla_q_g, mla_k_g, mla_w_out, mem_norm_g, mem_w_kv, mem_q_g, mem_k_g):
    bp, seq, d = x_prompt.shape
    bs, dec, _ = x_sample.shape
    depth = mem_norm_g.shape[0]
    n_mem = mem_prompt.shape[1]
    dc = conv_w.shape[1]
    q_lora = mla_q_lora_g.shape[0]
    kv_lora = mla_ckv_g.shape[0]
    past = page_table.shape[1] * PAGE_SIZE

    tm_p = _pick_tile(seq, 512)
    nb_s = _pick_tile(bs, SAMPLE_BATCHES_PER_STEP) if bs % SUBLANES == 0 else bs

    mem_k, mem_v = _mem_kv(mem_prompt, mem_norm_g, mem_w_kv, mem_k_g)
    mem_rows = n_mem * MEM_HEADS
    mem_k2 = mem_k.reshape(depth * bp, mem_rows, MEM_HEAD_DIM)
    mem_v2 = mem_v.reshape(depth * bp, mem_rows, MEM_HEAD_DIM)
    smem_k2 = cache_mem_k.reshape(depth * bs, mem_rows, MEM_HEAD_DIM)
    smem_v2 = cache_mem_v.reshape(depth * bs, mem_rows, MEM_HEAD_DIM)

    cw_in = conv_w_in.astype(_BF16)
    cw_out = conv_w_out.astype(_BF16)
    hist_p = jnp.zeros((bp, CONV_WIDTH - 1, dc), _F32)
    y_p, conv_p = _conv_layer(x_prompt, hist_p, mem_k2, mem_v2, 0, mem_q_g[0], conv_norm_g, cw_in, conv_w, cw_out,
                              nb=1, tm=tm_p)
    y_s, conv_s = _conv_layer(x_sample, state_conv, smem_k2, smem_v2, 0, mem_q_g[0], conv_norm_g, cw_in, conv_w,
                              cw_out, nb=nb_s, tm=dec)

    w_in, w_uq, q_gain, k_gain = _mla_weights(mla_w_in, mla_w_uq, mla_q_g, mla_k_g, q_lora, kv_lora)
    w_uk2 = mla_w_uk.reshape(kv_lora, N_HEADS * QK_NOPE).astype(_BF16)
    w_uv2 = mla_w_uv.reshape(kv_lora, N_HEADS * V_HEAD).astype(_BF16)
    w_out = mla_w_out.astype(_BF16)
    tab_p = _rope_table(jnp.arange(seq, dtype=jnp.int32))
    y_p, ckv_p, kpe_p, ksc_p = _mla_prompt(y_p, tab_p, mem_k2, mem_v2, 1, mem_q_g[1], mla_norm_g, w_in,
                                           mla_q_lora_g, w_uq, q_gain, mla_ckv_g, w_uk2, w_uv2, k_gain, w_out,
                                           tm=tm_p)

    n_s = bs * dec
    tab_s = jnp.tile(_rope_table(past + jnp.arange(dec, dtype=jnp.int32)), (bs, 1))
    w_ukt = jnp.transpose(mla_w_uk, (1, 2, 0)).astype(_BF16)
    w_uv3 = jnp.transpose(mla_w_uv, (1, 0, 2)).astype(_BF16)
    qabs, qpe, ckv_s, kpe_s, ksc_s, rest = _mla_sample_proj(
        y_s.reshape(n_s, d), tab_s, mla_norm_g, w_in, mla_q_lora_g, w_uq, q_gain, mla_ckv_g, w_uk2, w_ukt, k_gain,
        tm=_pick_tile(n_s, 512))
    rows = dec * N_HEADS
    olat = _mla_sample_attn(page_table, qabs.reshape(bs, rows, kv_lora), qpe.reshape(bs, rows, LANES),
                            ckv_s.reshape(bs, dec, kv_lora), kpe_s.reshape(bs, dec, QK_ROPE),
                            ksc_s.reshape(bs, dec, N_HEADS), cache_ckv, jnp.transpose(cache_kpe, (0, 2, 1)),
                            jnp.transpose(cache_kscale, (0, 2, 1)))
    y_s2 = _mla_sample_out(y_s.reshape(n_s, d), olat.reshape(n_s, N_HEADS * kv_lora), rest, smem_k2, smem_v2, 1,
                           mem_q_g[1], w_uv3, w_out, nb=nb_s, dec=dec)

    new_shape = (depth, bp, n_mem, MEM_HEADS, MEM_HEAD_DIM)
    return (y_p, y_s2.reshape(bs, dec, d), conv_p, conv_s, ckv_p, kpe_p, ksc_p,
            ckv_s.reshape(bs, dec, kv_lora), kpe_s.reshape(bs, dec, QK_ROPE), ksc_s.reshape(bs, dec, N_HEADS),
            mem_k.reshape(new_shape), mem_v.reshape(new_shape))
```

```python
import functools
import math

import jax
import jax.numpy as jnp
from jax import lax
from jax.experimental import pallas as pl
from jax.experimental.pallas import tpu as pltpu

EPS = 1e-6
ROPE_THETA = 10000.0
PAGE_SIZE = 128
N_HEADS = 8
QK_NOPE = 128
QK_ROPE = 64
QK_HEAD = QK_NOPE + QK_ROPE
V_HEAD = 128
MEM_HEADS = 4
MEM_HEAD_DIM = 128
D_MEM = MEM_HEADS * MEM_HEAD_DIM
CONV_WIDTH = 3
SM_SCALE = QK_HEAD ** -0.5
MEM_SCALE = MEM_HEAD_DIM ** -0.5

LANES = 128
SUBLANES = 8
HEAD_PAD = 2 * LANES
VMEM_LIMIT = 60000 * 1024
PAGES_PER_STEP = 32
SAMPLE_DMA_SLOTS = 4
SAMPLE_BATCHES_PER_STEP = 16
CONV_PROMPT_ROWS = 1024
MLA_PROMPT_ROWS = 512
ATTN_Q_ROWS = 256
LOG2E = math.log2(math.e)

_F32 = jnp.float32
_BF16 = jnp.bfloat16


def _dot(a, b):
    return jnp.dot(a, b, preferred_element_type=_F32)


def _dot_nt(a, b):
    return lax.dot_general(a, b, (((1,), (1,)), ((), ())), preferred_element_type=_F32)


def _rms(x, g):
    return x * lax.rsqrt(jnp.mean(x * x, axis=-1, keepdims=True) + EPS) * g


def _silu(z):
    h = 0.5 * z
    return h + h * jnp.tanh(h)


def _const_spec(shape):
    nd = len(shape)
    return pl.BlockSpec(shape, lambda *_: (0,) * nd, pipeline_mode=pl.Buffered(1))


def _params(n_axes):
    return pltpu.CompilerParams(dimension_semantics=("arbitrary",) * n_axes,
                                vmem_limit_bytes=VMEM_LIMIT)


def _mem_attend(qm, k_ref, v_ref, bi, qg):
    rows = qm.shape[0]
    mem_rows = k_ref.shape[1]
    n_mem = mem_rows // MEM_HEADS
    heads = [slice(h * MEM_HEAD_DIM, (h + 1) * MEM_HEAD_DIM) for h in range(MEM_HEADS)]
    if rows * MEM_HEADS <= LANES:
        q = jnp.concatenate([_rms(qm[:, sl], qg) * (MEM_SCALE * LOG2E) for sl in heads], axis=0).astype(_BF16)
        s = _dot_nt(q, k_ref[bi].astype(_BF16))
        q_head = lax.broadcasted_iota(jnp.int32, s.shape, 0) // rows
        k_head = lax.broadcasted_iota(jnp.int32, s.shape, 1) % MEM_HEADS
        s = jnp.where(q_head == k_head, s, -jnp.inf)
        p = jnp.exp2(s - jnp.max(s, axis=-1, keepdims=True))
        o = _dot(p.astype(_BF16), v_ref[bi].astype(_BF16)) / jnp.sum(p, axis=-1, keepdims=True)
        return [o[h * rows:(h + 1) * rows] for h in range(MEM_HEADS)]
    outs = []
    for sl in heads:
        h = sl.start // MEM_HEAD_DIM
        q = (_rms(qm[:, sl], qg) * (MEM_SCALE * LOG2E)).astype(_BF16)
        k = k_ref[bi, pl.ds(h, n_mem, stride=MEM_HEADS), :].astype(_BF16)
        v = v_ref[bi, pl.ds(h, n_mem, stride=MEM_HEADS), :].astype(_BF16)
        s = _dot_nt(q, k)
        p = jnp.exp2(s - jnp.max(s, axis=-1, keepdims=True))
        o = _dot(p.astype(_BF16), v) / jnp.sum(p, axis=-1, keepdims=True)
        outs.append(o)
    return outs


def _mem_kv_kernel(mem_ref, ng_ref, w_ref, kg_ref, k_out, v_out):
    x = mem_ref[0]
    xn = _rms(x, ng_ref[0]).astype(_BF16)
    kv = _dot(xn, w_ref[0])
    kg = kg_ref[0]
    m = x.shape[0]
    for h in range(MEM_HEADS):
        sl = slice(h * MEM_HEAD_DIM, (h + 1) * MEM_HEAD_DIM)
        rows = pl.ds(h, m, stride=MEM_HEADS)
        k_out[0, 0, rows, :] = _rms(kv[:, sl], kg)
        v_out[0, 0, rows, :] = kv[:, D_MEM + h * MEM_HEAD_DIM:D_MEM + (h + 1) * MEM_HEAD_DIM]


def _mem_kv(mem, norm_g, w_kv, k_g):
    b, m, d = mem.shape
    depth = norm_g.shape[0]
    out = jax.ShapeDtypeStruct((depth, b, m * MEM_HEADS, MEM_HEAD_DIM), _F32)
    return pl.pallas_call(
        _mem_kv_kernel,
        grid=(depth, b),
        in_specs=[
            pl.BlockSpec((1, m, d), lambda l, i: (i, 0, 0)),
            pl.BlockSpec((1, 1, d), lambda l, i: (l, 0, 0)),
            pl.BlockSpec((1, d, 2 * D_MEM), lambda l, i: (l, 0, 0)),
            pl.BlockSpec((1, 1, MEM_HEAD_DIM), lambda l, i: (l, 0, 0)),
        ],
        out_specs=[pl.BlockSpec((1, 1, m * MEM_HEADS, MEM_HEAD_DIM), lambda l, i: (l, i, 0, 0))] * 2,
        out_shape=[out, out],
        compiler_params=_params(2),
        name="mem_kv",
    )(mem, norm_g.reshape(depth, 1, d), w_kv.astype(_BF16), k_g.reshape(depth, 1, MEM_HEAD_DIM))


def _conv_layer_kernel(x_ref, hist_ref, mk_ref, mv_ref, ng_ref, win_ref, cw_ref, wout_ref, qg_ref,
                       y_ref, state_ref, cbuf, obuf, *, nb, tm, dc):
    rows = nb * tm
    d = x_ref.shape[-1]

    @pl.when(pl.program_id(1) == 0)
    def _():
        cbuf[:, SUBLANES - 2:SUBLANES, :] = hist_ref[...]

    x = x_ref[...].reshape(rows, d)
    xn = _rms(x, ng_ref[...]).astype(_BF16)
    proj = _dot(xn, win_ref[...])
    cu = proj[:, 2 * dc:3 * dc] * proj[:, :dc]
    cbuf[:, SUBLANES:SUBLANES + tm, :] = cu.reshape(nb, tm, dc)
    cw = cw_ref[...]
    conv = (cw[0:1] * cbuf[:, SUBLANES - 2:SUBLANES - 2 + tm, :].reshape(rows, dc)
            + cw[1:2] * cbuf[:, SUBLANES - 1:SUBLANES - 1 + tm, :].reshape(rows, dc)
            + cw[2:3] * cu)
    state = cbuf[:, tm + SUBLANES - 2:tm + SUBLANES, :]
    state_ref[...] = state
    cbuf[:, SUBLANES - 2:SUBLANES, :] = state
    zoff = 3 * dc + D_MEM
    obuf[:, :dc] = (proj[:, dc:2 * dc] * conv * _silu(proj[:, zoff:zoff + dc])).astype(_BF16)
    qg = qg_ref[...]
    for bi in range(nb):
        r = slice(bi * tm, (bi + 1) * tm)
        outs = _mem_attend(proj[r, 3 * dc:3 * dc + D_MEM], mk_ref, mv_ref, bi, qg)
        for h, o in enumerate(outs):
            c0 = dc + h * MEM_HEAD_DIM
            obuf[r, c0:c0 + MEM_HEAD_DIM] = (o * _silu(proj[r, zoff + c0:zoff + c0 + MEM_HEAD_DIM])).astype(_BF16)
    y_ref[...] = (x + _dot(obuf[...], wout_ref[...])).reshape(nb, tm, d)


def _conv_layer(x, hist, mem_k, mem_v, mem_layer, mem_q_g, norm_g, w_in, conv_w, w_out, *, nb, tm):
    b, t, d = x.shape
    dc = conv_w.shape[1]
    mem_rows = mem_k.shape[1]
    off = mem_layer * (b // nb)
    kern = functools.partial(_conv_layer_kernel, nb=nb, tm=tm, dc=dc)
    return pl.pallas_call(
        kern,
        grid=(b // nb, t // tm),
        in_specs=[
            pl.BlockSpec((nb, tm, d), lambda i, j: (i, j, 0)),
            pl.BlockSpec((nb, CONV_WIDTH - 1, dc), lambda i, j: (i, 0, 0)),
            pl.BlockSpec((nb, mem_rows, MEM_HEAD_DIM), lambda i, j: (off + i, 0, 0)),
            pl.BlockSpec((nb, mem_rows, MEM_HEAD_DIM), lambda i, j: (off + i, 0, 0)),
            _const_spec((1, d)),
            _const_spec(w_in.shape),
            _const_spec(conv_w.shape),
            _const_spec(w_out.shape),
            _const_spec((1, MEM_HEAD_DIM)),
        ],
        out_specs=[
            pl.BlockSpec((nb, tm, d), lambda i, j: (i, j, 0)),
            pl.BlockSpec((nb, CONV_WIDTH - 1, dc), lambda i, j: (i, 0, 0)),
        ],
        out_shape=[jax.ShapeDtypeStruct((b, t, d), _F32),
                   jax.ShapeDtypeStruct((b, CONV_WIDTH - 1, dc), _F32)],
        scratch_shapes=[pltpu.VMEM((nb, tm + SUBLANES, dc), _F32),
                        pltpu.VMEM((nb * tm, dc + D_MEM), _BF16)],
        compiler_params=_params(2),
        name="conv_layer",
    )(x, hist, mem_k, mem_v, norm_g.reshape(1, d), w_in, conv_w, w_out, mem_q_g.reshape(1, MEM_HEAD_DIM))


def _rope_table(pos):
    inv = 1.0 / (ROPE_THETA ** (jnp.arange(0, QK_ROPE, 2, dtype=_F32) / QK_ROPE))
    ang = pos.astype(_F32)[:, None] * inv[None, :]
    c, s = jnp.cos(ang), jnp.sin(ang)
    return jnp.concatenate([c, c, -s, s], axis=-1)


def _swap_halves(w):
    half = w.shape[-1] // 2
    return jnp.concatenate([w[..., half:], w[..., :half]], axis=-1)


def _mla_weights(mla_w_in, mla_w_uq, mla_q_g, mla_k_g, q_lora, kv_lora):
    o = q_lora + kv_lora
    kpe_w = mla_w_in[:, o:o + QK_ROPE]
    w_in = jnp.concatenate([mla_w_in[:, :o], kpe_w, _swap_halves(kpe_w), mla_w_in[:, o + QK_ROPE:]], axis=1)
    wq = mla_w_uq.reshape(q_lora, N_HEADS, QK_HEAD)
    wq = jnp.concatenate([wq, _swap_halves(wq[..., QK_NOPE:])], axis=-1)
    w_uq = wq.reshape(q_lora, N_HEADS * HEAD_PAD)
    g_rope = mla_q_g[QK_NOPE:]
    q_gain = jnp.concatenate([mla_q_g[:QK_NOPE] * mla_k_g[:QK_NOPE], g_rope, _swap_halves(g_rope)])
    k_rope = mla_k_g[QK_NOPE:]
    k_gain = jnp.concatenate([k_rope, _swap_halves(k_rope)])
    return (w_in.astype(_BF16), w_uq.astype(_BF16), q_gain.reshape(1, HEAD_PAD), k_gain.reshape(1, LANES))


def _rope_lane_mask(rows):
    return lax.broadcasted_iota(jnp.int32, (rows, LANES), 1) < QK_ROPE


def _q_heads(cq, qlg, wuq_ref, q_gain, tab, rope_mask):
    cqn = _rms(cq, qlg).astype(_BF16)
    for h in range(N_HEADS):
        q = _dot(cqn, wuq_ref[:, h * HEAD_PAD:(h + 1) * HEAD_PAD])
        a = q[:, :LANES]
        b = q[:, LANES:]
        ss = jnp.sum(a * a + jnp.where(rope_mask, b * b, 0.0), axis=-1, keepdims=True)
        r = lax.rsqrt(ss / QK_HEAD + EPS)
        yield a * r * q_gain[:, :LANES], b * r * q_gain[:, LANES:] * tab


def _kpe_rot(kpe_blk, k_gain, tab):
    kt = kpe_blk * k_gain * tab
    return kt + pltpu.roll(kt, QK_ROPE, axis=1)


def _head_select(cols, rows):
    lane = lax.broadcasted_iota(jnp.int32, (rows, N_HEADS), 1)
    out = jnp.zeros((rows, N_HEADS), _F32)
    for h, c in enumerate(cols):
        out = jnp.where(lane == h, c, out)
    return out


def _mla_prompt_kernel(x_ref, tab_ref, mk_ref, mv_ref, ng_ref, win_ref, qlg_ref, wuq_ref, qgain_ref,
                       ckvg_ref, wuk_ref, wuv_ref, kgain_ref, wout_ref, qg_ref,
                       y_ref, ckv_ref, kpe_ref, ksc_ref,
                       late, qbuf, kbuf, vbuf, obuf, *, tm, tq, q_lora, kv_lora):
    i = pl.program_id(1)
    x = x_ref[0]
    tab = tab_ref[...]
    rope_mask = _rope_lane_mask(tm)
    xn = _rms(x, ng_ref[...]).astype(_BF16)
    o = q_lora + kv_lora
    early = _dot(xn, win_ref[:, :o + LANES])
    late[...] = _dot(xn, win_ref[:, o + LANES:])

    heads = _q_heads(early[:, :q_lora], qlg_ref[...], wuq_ref, qgain_ref[...] * (SM_SCALE * LOG2E), tab, rope_mask)
    for h, (a, b) in enumerate(heads):
        qbuf[h, :, :LANES] = a.astype(_BF16)
        qbuf[h, :, LANES:] = b.astype(_BF16)

    ckv = _rms(early[:, q_lora:o], ckvg_ref[...])
    ckv_ref[0] = ckv
    ckv_b = ckv.astype(_BF16)
    kpe_blk = early[:, o:o + LANES]
    skpe = jnp.sum(jnp.where(rope_mask, kpe_blk * kpe_blk, 0.0), axis=-1, keepdims=True)
    rot = _kpe_rot(kpe_blk, kgain_ref[...], tab)
    kpe_ref[0] = rot[:, :QK_ROPE]
    row0 = pl.multiple_of(i * tm, tm)
    kscs = []
    for h2 in range(0, N_HEADS, 2):
        k_pair = _dot(ckv_b, wuk_ref[:, h2 * QK_NOPE:(h2 + 2) * QK_NOPE])
        v_pair = _dot(ckv_b, wuv_ref[:, h2 * V_HEAD:(h2 + 2) * V_HEAD])
        for h in (h2, h2 + 1):
            kn = k_pair[:, (h - h2) * QK_NOPE:(h - h2 + 1) * QK_NOPE]
            ksc = lax.rsqrt((jnp.sum(kn * kn, axis=-1, keepdims=True) + skpe) / QK_HEAD + EPS)
            kscs.append(ksc)
            kbuf[h, pl.ds(row0, tm), :LANES] = (kn * ksc).astype(_BF16)
            kbuf[h, pl.ds(row0, tm), LANES:] = (rot * ksc).astype(_BF16)
            vbuf[h, pl.ds(row0, tm), :V_HEAD] = v_pair[:, (h - h2) * V_HEAD:(h - h2 + 1) * V_HEAD].astype(_BF16)
            vbuf[h, pl.ds(row0, tm), V_HEAD:] = jnp.ones((tm, LANES), _BF16)
    ksc_ref[0] = _head_select(kscs, tm)

    causal = (lax.broadcasted_iota(jnp.int32, (tq, tq), 0) >= lax.broadcasted_iota(jnp.int32, (tq, tq), 1))

    def attend_block(h, r0, n_keys):
        q = qbuf[h, r0:r0 + tq, :]
        s = _dot_nt(q, kbuf[h, 0:n_keys, :])
        diag = jnp.where(causal, s[:, n_keys - tq:], -jnp.inf)
        m = jnp.max(diag, axis=-1, keepdims=True)
        if n_keys > tq:
            past = s[:, :n_keys - tq]
            m = jnp.maximum(m, jnp.max(past, axis=-1, keepdims=True))
            p = jnp.concatenate([jnp.exp2(past - m), jnp.exp2(diag - m)], axis=1)
        else:
            p = jnp.exp2(diag - m)
        ov = _dot(p.astype(_BF16), vbuf[h, 0:n_keys, :])
        cols = slice(h * V_HEAD, (h + 1) * V_HEAD)
        gate = _silu(late[r0:r0 + tq, zoff + h * V_HEAD:zoff + (h + 1) * V_HEAD])
        obuf[r0:r0 + tq, cols] = (ov[:, :V_HEAD] / ov[:, V_HEAD:] * gate).astype(_BF16)

    d_mla = N_HEADS * V_HEAD
    zoff = D_MEM
    for ti in range(kbuf.shape[1] // tm):
        @pl.when(i == ti)
        def _(ti=ti):
            for r0 in range(0, tm, tq):
                for h in range(N_HEADS):
                    attend_block(h, r0, ti * tm + r0 + tq)

    outs = _mem_attend(late[:, :D_MEM], mk_ref, mv_ref, 0, qg_ref[...])
    for h, om in enumerate(outs):
        c0 = d_mla + h * MEM_HEAD_DIM
        obuf[:, c0:c0 + MEM_HEAD_DIM] = (om * _silu(late[:, zoff + c0:zoff + c0 + MEM_HEAD_DIM])).astype(_BF16)
    y_ref[0] = x + _dot(obuf[...], wout_ref[...])


def _mla_prompt(x, tab, mem_k, mem_v, mem_layer, mem_q_g, norm_g, w_in, q_lora_g, w_uq, q_gain, ckv_g,
                w_uk, w_uv, k_gain, w_out, *, tm):
    b, t, d = x.shape
    q_lora = q_lora_g.shape[0]
    kv_lora = ckv_g.shape[0]
    mem_rows = mem_k.shape[1]
    off = mem_layer * b
    d_out = N_HEADS * V_HEAD + D_MEM
    tq = math.gcd(tm, ATTN_Q_ROWS)
    kern = functools.partial(_mla_prompt_kernel, tm=tm, tq=tq, q_lora=q_lora, kv_lora=kv_lora)
    tok = lambda w: pl.BlockSpec((1, tm, w), lambda i, j: (i, j, 0))
    return pl.pallas_call(
        kern,
        grid=(b, t // tm),
        in_specs=[
            tok(d),
            pl.BlockSpec((tm, LANES), lambda i, j: (j, 0)),
            pl.BlockSpec((1, mem_rows, MEM_HEAD_DIM), lambda i, j: (off + i, 0, 0), pipeline_mode=pl.Buffered(1)),
            pl.BlockSpec((1, mem_rows, MEM_HEAD_DIM), lambda i, j: (off + i, 0, 0), pipeline_mode=pl.Buffered(1)),
            _const_spec((1, d)),
            _const_spec(w_in.shape),
            _const_spec((1, q_lora)),
            _const_spec(w_uq.shape),
            _const_spec((1, HEAD_PAD)),
            _const_spec((1, kv_lora)),
            _const_spec(w_uk.shape),
            _const_spec(w_uv.shape),
            _const_spec((1, LANES)),
            _const_spec(w_out.shape),
            _const_spec((1, MEM_HEAD_DIM)),
        ],
        out_specs=[tok(d), tok(kv_lora), tok(QK_ROPE), tok(N_HEADS)],
        out_shape=[jax.ShapeDtypeStruct((b, t, d), _F32),
                   jax.ShapeDtypeStruct((b, t, kv_lora), _F32),
                   jax.ShapeDtypeStruct((b, t, QK_ROPE), _F32),
                   jax.ShapeDtypeStruct((b, t, N_HEADS), _F32)],
        scratch_shapes=[
            pltpu.VMEM((tm, D_MEM + d_out), _F32),
            pltpu.VMEM((N_HEADS, tm, HEAD_PAD), _BF16),
            pltpu.VMEM((N_HEADS, t, HEAD_PAD), _BF16),
            pltpu.VMEM((N_HEADS, t, V_HEAD + LANES), _BF16),
            pltpu.VMEM((tm, d_out), _BF16),
        ],
        compiler_params=_params(2),
        name="mla_prompt",
    )(x, tab, mem_k, mem_v, norm_g.reshape(1, d), w_in, q_lora_g.reshape(1, q_lora), w_uq, q_gain,
      ckv_g.reshape(1, kv_lora), w_uk, w_uv, k_gain, w_out, mem_q_g.reshape(1, MEM_HEAD_DIM))


def _mla_sample_proj_kernel(x_ref, tab_ref, ng_ref, win_ref, qlg_ref, wuq_ref, qgain_ref, ckvg_ref,
                            wuk_ref, wukt_ref, kgain_ref,
                            qabs_ref, qpe_ref, ckv_ref, kpe_ref, ksc_ref, rest_ref, *, q_lora, kv_lora):
    rows = x_ref.shape[0]
    tab = tab_ref[...]
    rope_mask = _rope_lane_mask(rows)
    xn = _rms(x_ref[...], ng_ref[...]).astype(_BF16)
    proj = _dot(xn, win_ref[...])
    o = q_lora + kv_lora
    rest_ref[...] = proj[:, o + LANES:]

    heads = _q_heads(proj[:, :q_lora], qlg_ref[...], wuq_ref, qgain_ref[...] * (SM_SCALE * LOG2E), tab, rope_mask)
    for h, (a, b) in enumerate(heads):
        qabs_ref[:, h * kv_lora:(h + 1) * kv_lora] = _dot(a.astype(_BF16), wukt_ref[h]).astype(_BF16)
        qpe_ref[:, h * LANES:(h + 1) * LANES] = (b + pltpu.roll(b, QK_ROPE, axis=1)).astype(_BF16)

    ckv = _rms(proj[:, q_lora:o], ckvg_ref[...])
    ckv_ref[...] = ckv
    k_nope = _dot(ckv.astype(_BF16), wuk_ref[...])
    kpe_blk = proj[:, o:o + LANES]
    skpe = jnp.sum(jnp.where(rope_mask, kpe_blk * kpe_blk, 0.0), axis=-1, keepdims=True)
    kpe_ref[...] = _kpe_rot(kpe_blk, kgain_ref[...], tab)[:, :QK_ROPE]
    kscs = []
    for h in range(N_HEADS):
        kn = k_nope[:, h * QK_NOPE:(h + 1) * QK_NOPE]
        kscs.append(lax.rsqrt((jnp.sum(kn * kn, axis=-1, keepdims=True) + skpe) / QK_HEAD + EPS))
    ksc_ref[...] = _head_select(kscs, rows)


def _mla_sample_proj(x2, tab, norm_g, w_in, q_lora_g, w_uq, q_gain, ckv_g, w_uk, w_ukt, k_gain, *, tm):
    n, d = x2.shape
    q_lora = q_lora_g.shape[0]
    kv_lora = ckv_g.shape[0]
    n_rest = w_in.shape[1] - (q_lora + kv_lora + LANES)
    kern = functools.partial(_mla_sample_proj_kernel, q_lora=q_lora, kv_lora=kv_lora)
    tok = lambda w: pl.BlockSpec((tm, w), lambda i: (i, 0))
    return pl.pallas_call(
        kern,
        grid=(n // tm,),
        in_specs=[
            tok(d), tok(LANES),
            _const_spec((1, d)),
            _const_spec(w_in.shape),
            _const_spec((1, q_lora)),
            _const_spec(w_uq.shape),
            _const_spec((1, HEAD_PAD)),
            _const_spec((1, kv_lora)),
            _const_spec(w_uk.shape),
            _const_spec(w_ukt.shape),
            _const_spec((1, LANES)),
        ],
        out_specs=[tok(N_HEADS * kv_lora), tok(N_HEADS * LANES), tok(kv_lora), tok(QK_ROPE), tok(N_HEADS),
                   tok(n_rest)],
        out_shape=[jax.ShapeDtypeStruct((n, N_HEADS * kv_lora), _BF16),
                   jax.ShapeDtypeStruct((n, N_HEADS * LANES), _BF16),
                   jax.ShapeDtypeStruct((n, kv_lora), _F32),
                   jax.ShapeDtypeStruct((n, QK_ROPE), _F32),
                   jax.ShapeDtypeStruct((n, N_HEADS), _F32),
                   jax.ShapeDtypeStruct((n, n_rest), _F32)],
        compiler_params=_params(1),
        name="mla_sample_proj",
    )(x2, tab, norm_g.reshape(1, d), w_in, q_lora_g.reshape(1, q_lora), w_uq, q_gain,
      ckv_g.reshape(1, kv_lora), w_uk, w_ukt, k_gain)


def _mla_sample_attn_kernel(pt_ref, qabs_ref, qpe_ref, nckv_ref, nkpe_ref, nksc_ref, ckv_hbm, kpe_hbm, ksc_hbm,
                            o_ref, ckv_buf, kpe_buf, ksc_buf, sems, ck_all, s_all, *, pps, n_blocks, n_slots, dec):
    b = pl.program_id(0)
    rows = N_HEADS * dec
    blk_keys = pps * PAGE_SIZE
    n_keys = n_blocks * blk_keys
    ahead = n_slots - 1

    def page_copies(bb, blk, slot):
        copies = []
        for p in range(pps):
            page = pt_ref[bb, blk * pps + p]
            keys = pl.ds(p * PAGE_SIZE, PAGE_SIZE)
            copies.append(pltpu.make_async_copy(ckv_hbm.at[page], ckv_buf.at[slot, keys, :], sems.at[0, slot]))
            copies.append(pltpu.make_async_copy(kpe_hbm.at[page], kpe_buf.at[slot, p], sems.at[1, slot]))
            copies.append(pltpu.make_async_copy(ksc_hbm.at[page], ksc_buf.at[slot, p], sems.at[2, slot]))
        return copies

    def slot_of(rel):
        return rel % n_slots if n_blocks % n_slots == 0 else lax.rem(b * n_blocks + rel, n_slots)

    def start_block(rel):
        db, blk = divmod(rel, n_blocks)

        def go():
            for c in page_copies(b + db, blk, slot_of(rel)):
                c.start()

        if db == 0:
            go()
        else:
            pl.when(b + db < pl.num_programs(0))(go)

    @pl.when(b == 0)
    def _():
        for rel in range(ahead):
            start_block(rel)

    qa = qabs_ref[0]
    qp = qpe_ref[0][:, :QK_ROPE]

    mx = jnp.full((rows, LANES), -jnp.inf, _F32)
    for blk in range(n_blocks):
        start_block(blk + ahead)
        slot = slot_of(blk)
        for c in page_copies(b, blk, slot):
            c.wait()
        ck = ckv_buf[slot].astype(_BF16)
        ck_all[blk * blk_keys:(blk + 1) * blk_keys, :] = ck
        kpt = jnp.concatenate([kpe_buf[slot, p] for p in range(pps)], axis=1).astype(_BF16)
        kst = jnp.concatenate([ksc_buf[slot, p] for p in range(pps)], axis=1)
        s = _dot_nt(qa, ck) + _dot(qp, kpt)
        s = (s.reshape(dec, N_HEADS, blk_keys) * kst[None]).reshape(rows, blk_keys)
        s_all[:, blk * blk_keys:(blk + 1) * blk_keys] = s
        for t in range(blk_keys // LANES):
            mx = jnp.maximum(mx, s[:, t * LANES:(t + 1) * LANES])

    nckv = nckv_ref[0].astype(_BF16)
    rr = lax.broadcasted_iota(jnp.int32, (rows, N_HEADS), 0)
    hh = lax.broadcasted_iota(jnp.int32, (rows, N_HEADS), 1)
    expand = (rr % N_HEADS == hh).astype(_F32)
    scale = lax.dot_general(expand, nksc_ref[0], (((1,), (1,)), ((), ())), precision=lax.Precision.HIGHEST,
                            preferred_element_type=_F32)
    s_new = (_dot_nt(qa, nckv) + _dot_nt(qp, nkpe_ref[0].astype(_BF16))) * scale
    tq = lax.broadcasted_iota(jnp.int32, (rows, dec), 0) // N_HEADS
    tk = lax.broadcasted_iota(jnp.int32, (rows, dec), 1)
    s_new = jnp.where(tk <= tq, s_new, -jnp.inf)

    m = jnp.maximum(jnp.max(mx, axis=-1, keepdims=True), jnp.max(s_new, axis=-1, keepdims=True))
    p_new = jnp.exp2(s_new - m)
    p = jnp.exp2(s_all[...] - m)
    l = jnp.sum(p, axis=-1, keepdims=True) + jnp.sum(p_new, axis=-1, keepdims=True)
    acc = _dot(p.astype(_BF16), ck_all[...]) + _dot(p_new.astype(_BF16), nckv)
    o_ref[0] = acc / l


def _mla_sample_attn(page_table, qabs, qpe, nckv, nkpe, nksc, cache_ckv, cache_kpe_t, cache_ksc_t):
    b, rows, kv_lora = qabs.shape
    dec = nckv.shape[1]
    n_pages = page_table.shape[1]
    pps = math.gcd(n_pages, PAGES_PER_STEP)
    n_blocks = n_pages // pps
    blk_keys = pps * PAGE_SIZE
    n_keys = n_pages * PAGE_SIZE
    n_slots = SAMPLE_DMA_SLOTS
    kern = functools.partial(_mla_sample_attn_kernel, pps=pps, n_blocks=n_blocks, n_slots=n_slots, dec=dec)
    per_b = lambda shape: pl.BlockSpec((1,) + shape, lambda i, pt: (i, 0, 0))
    hbm = pl.BlockSpec(memory_space=pl.ANY)
    grid_spec = pltpu.PrefetchScalarGridSpec(
        num_scalar_prefetch=1,
        grid=(b,),
        in_specs=[per_b((rows, kv_lora)), per_b((rows, LANES)), per_b((dec, kv_lora)), per_b((dec, QK_ROPE)),
                  per_b((dec, N_HEADS)), hbm, hbm, hbm],
        out_specs=per_b((rows, kv_lora)),
        scratch_shapes=[
            pltpu.VMEM((n_slots, blk_keys, kv_lora), _F32),
            pltpu.VMEM((n_slots, pps, QK_ROPE, PAGE_SIZE), _F32),
            pltpu.VMEM((n_slots, pps, N_HEADS, PAGE_SIZE), _F32),
            pltpu.SemaphoreType.DMA((3, n_slots)),
            pltpu.VMEM((n_keys, kv_lora), _BF16),
            pltpu.VMEM((rows, n_keys), _F32),
        ],
    )
    return pl.pallas_call(
        kern,
        grid_spec=grid_spec,
        out_shape=jax.ShapeDtypeStruct((b, rows, kv_lora), _F32),
        compiler_params=_params(1),
        name="mla_sample_attn",
    )(page_table, qabs, qpe, nckv, nkpe, nksc, cache_ckv, cache_kpe_t, cache_ksc_t)


def _mla_sample_out_kernel(x_ref, olat_ref, rest_ref, mk_ref, mv_ref, wuv_ref, wout_ref, qg_ref,
                           y_ref, obuf, *, nb, dec, kv_lora):
    d_mla = N_HEADS * V_HEAD
    zoff = D_MEM
    for h in range(N_HEADS):
        c0 = h * V_HEAD
        y_h = _dot(olat_ref[:, h * kv_lora:(h + 1) * kv_lora].astype(_BF16), wuv_ref[h])
        obuf[:, c0:c0 + V_HEAD] = (y_h * _silu(rest_ref[:, zoff + c0:zoff + c0 + V_HEAD])).astype(_BF16)
    qg = qg_ref[...]
    for bi in range(nb):
        r = slice(bi * dec, (bi + 1) * dec)
        outs = _mem_attend(rest_ref[r, :D_MEM], mk_ref, mv_ref, bi, qg)
        for h, om in enumerate(outs):
            c0 = d_mla + h * MEM_HEAD_DIM
            obuf[r, c0:c0 + MEM_HEAD_DIM] = (om * _silu(rest_ref[r, zoff + c0:zoff + c0 + MEM_HEAD_DIM])).astype(_BF16)
    y_ref[...] = x_ref[...] + _dot(obuf[...], wout_ref[...])


def _mla_sample_out(x2, olat, rest, mem_k, mem_v, mem_layer, mem_q_g, w_uv3, w_out, *, nb, dec):
    n, d = x2.shape
    kv_lora = w_uv3.shape[1]
    mem_rows = mem_k.shape[1]
    rows = nb * dec
    off = mem_layer * (n // rows)
    kern = functools.partial(_mla_sample_out_kernel, nb=nb, dec=dec, kv_lora=kv_lora)
    tok = lambda w: pl.BlockSpec((rows, w), lambda i: (i, 0))
    return pl.pallas_call(
        kern,
        grid=(n // rows,),
        in_specs=[
            tok(d), tok(olat.shape[1]), tok(rest.shape[1]),
            pl.BlockSpec((nb, mem_rows, MEM_HEAD_DIM), lambda i: (off + i, 0, 0)),
            pl.BlockSpec((nb, mem_rows, MEM_HEAD_DIM), lambda i: (off + i, 0, 0)),
            _const_spec(w_uv3.shape),
            _const_spec(w_out.shape),
            _const_spec((1, MEM_HEAD_DIM)),
        ],
        out_specs=tok(d),
        out_shape=jax.ShapeDtypeStruct((n, d), _F32),
        scratch_shapes=[pltpu.VMEM((rows, w_out.shape[0]), _BF16)],
        compiler_params=_params(1),
        name="mla_sample_out",
    )(x2, olat, rest, mem_k, mem_v, w_uv3, w_out, mem_q_g.reshape(1, MEM_HEAD_DIM))


def _pick_tile(n, target):
    t = min(n, target)
    while n % t or t % SUBLANES:
        t -= 1
    return t


def kernel(x_prompt, x_sample, mem_prompt, state_conv, cache_ckv, cache_kpe, cache_kscale, cache_mem_k, cache_mem_v, page_table, conv_norm_g, conv_w_in, conv_w, conv_w_out, mla_norm_g, mla_w_in, mla_q_lora_g, mla_w_uq, mla_ckv_g, mla_w_uk, mla_w_uv, mla_q_g, mla_k_g, mla_w_out, mem_norm_g, mem_w_kv, mem_q_g, mem_k_g):
    bp, seq, d = x_prompt.shape
    bs, dec, _ = x_sample.shape
    depth = mem_norm_g.shape[0]
    n_mem = mem_prompt.shape[1]
    dc = conv_w.shape[1]
    q_lora = mla_q_lora_g.shape[0]
    kv_lora = mla_ckv_g.shape[0]
    past = page_table.shape[1] * PAGE_SIZE

    tm_p = _pick_tile(seq, MLA_PROMPT_ROWS)
    tm_c = _pick_tile(seq, CONV_PROMPT_ROWS)
    nb_s = _pick_tile(bs, SAMPLE_BATCHES_PER_STEP) if bs % SUBLANES == 0 else bs

    mem_k, mem_v = _mem_kv(mem_prompt, mem_norm_g, mem_w_kv, mem_k_g)
    mem_rows = n_mem * MEM_HEADS
    mem_k2 = mem_k.reshape(depth * bp, mem_rows, MEM_HEAD_DIM)
    mem_v2 = mem_v.reshape(depth * bp, mem_rows, MEM_HEAD_DIM)
    smem_k2 = cache_mem_k.reshape(depth * bs, mem_rows, MEM_HEAD_DIM)
    smem_v2 = cache_mem_v.reshape(depth * bs, mem_rows, MEM_HEAD_DIM)

    cw_in = conv_w_in.astype(_BF16)
    cw_out = conv_w_out.astype(_BF16)
    hist_p = jnp.zeros((bp, CONV_WIDTH - 1, dc), _F32)
    y_p, conv_p = _conv_layer(x_prompt, hist_p, mem_k2, mem_v2, 0, mem_q_g[0], conv_norm_g, cw_in, conv_w, cw_out,
                              nb=1, tm=tm_c)
    y_s, conv_s = _conv_layer(x_sample, state_conv, smem_k2, smem_v2, 0, mem_q_g[0], conv_norm_g, cw_in, conv_w,
                              cw_out, nb=nb_s, tm=dec)

    w_in, w_uq, q_gain, k_gain = _mla_weights(mla_w_in, mla_w_uq, mla_q_g, mla_k_g, q_lora, kv_lora)
    w_uk2 = mla_w_uk.reshape(kv_lora, N_HEADS * QK_NOPE).astype(_BF16)
    w_uv2 = mla_w_uv.reshape(kv_lora, N_HEADS * V_HEAD).astype(_BF16)
    w_out = mla_w_out.astype(_BF16)
    tab_p = _rope_table(jnp.arange(seq, dtype=jnp.int32))
    y_p, ckv_p, kpe_p, ksc_p = _mla_prompt(y_p, tab_p, mem_k2, mem_v2, 1, mem_q_g[1], mla_norm_g, w_in,
                                           mla_q_lora_g, w_uq, q_gain, mla_ckv_g, w_uk2, w_uv2, k_gain, w_out,
                                           tm=tm_p)

    n_s = bs * dec
    tab_s = jnp.tile(_rope_table(past + jnp.arange(dec, dtype=jnp.int32)), (bs, 1))
    w_ukt = jnp.transpose(mla_w_uk, (1, 2, 0)).astype(_BF16)
    w_uv3 = jnp.transpose(mla_w_uv, (1, 0, 2)).astype(_BF16)
    qabs, qpe, ckv_s, kpe_s, ksc_s, rest = _mla_sample_proj(
        y_s.reshape(n_s, d), tab_s, mla_norm_g, w_in, mla_q_lora_g, w_uq, q_gain, mla_ckv_g, w_uk2, w_ukt, k_gain,
        tm=_pick_tile(n_s, 512))
    rows = dec * N_HEADS
    olat = _mla_sample_attn(page_table, qabs.reshape(bs, rows, kv_lora), qpe.reshape(bs, rows, LANES),
                            ckv_s.reshape(bs, dec, kv_lora), kpe_s.reshape(bs, dec, QK_ROPE),
                            ksc_s.reshape(bs, dec, N_HEADS), cache_ckv, jnp.transpose(cache_kpe, (0, 2, 1)),
                            jnp.transpose(cache_kscale, (0, 2, 1)))
    y_s2 = _mla_sample_out(y_s.reshape(n_s, d), olat.reshape(n_s, N_HEADS * kv_lora), rest, smem_k2, smem_v2, 1,
                           mem_q_g[1], w_uv3, w_out, nb=nb_s, dec=dec)

    new_shape = (depth, bp, n_mem, MEM_HEADS, MEM_HEAD_DIM)
    return (y_p, y_s2.reshape(bs, dec, d), conv_p, conv_s, ckv_p, kpe_p, ksc_p,
            ckv_s.reshape(bs, dec, kv_lora), kpe_s.reshape(bs, dec, QK_ROPE), ksc_s.reshape(bs, dec, N_HEADS),
            mem_k.reshape(new_shape), mem_v.reshape(new_shape))
```
